```python
import math
import jax
import jax.numpy as jnp
from jax import lax
import numpy as np

D_MODEL = 2048
BATCH = 4
SEQ = 4096
DEPTH = 2
DEC_BATCH = 4
DEC_SEQ = 2048
PAST_LEN = 128

HEAD_DIM = 128
D_MIX = D_MODEL
GDN_HEADS = D_MIX // (2 * HEAD_DIM)
GDN_W = GDN_HEADS * HEAD_DIM
ATT_HEADS = (D_MIX - GDN_W) // HEAD_DIM
ATT_KV_HEADS = 2
ATT_GROUPS = ATT_HEADS // ATT_KV_HEADS
ATT_W = ATT_HEADS * HEAD_DIM
KV_W = ATT_KV_HEADS * HEAD_DIM
CONV_K = 5
CHUNK = 64
Q_BLOCK = 128
GRID_W = 64
ROPE_THETA = 10000.0
ROT_HALF = HEAD_DIM // 2
D_FF = 4 * D_MODEL
N_MOD = 6
GDN_IN = 4 * GDN_W + 4 * GDN_HEADS
IN_COLS = GDN_IN + ATT_W + 2 * KV_W
EPS = 1e-6

kernel_name = 'hybrid_gdn_gqa_axial_adaln_encoder'


def rmsnorm(x, w):
    xf = x.astype(jnp.float32)
    y = xf * lax.rsqrt(jnp.mean(xf * xf, axis=-1, keepdims=True) + EPS)
    return y * w.astype(jnp.float32)


def l2norm(x):
    return x * lax.rsqrt(jnp.sum(x * x, axis=-1, keepdims=True) + EPS)


def short_conv(x, w):
    C = x.shape[-1]
    return lax.conv_general_dilated(
        x, w[:, None, :].astype(x.dtype), window_strides=(1,),
        padding=[(CONV_K // 2, CONV_K // 2)],
        dimension_numbers=('NWC', 'WIO', 'NWC'), feature_group_count=C)


def gated_delta_chunked(q, k, v, g, beta):
    B, H, T, DK = q.shape
    DV = v.shape[-1]
    N = T // CHUNK
    q = q * DK ** -0.5

    def chunks(t):
        return t.reshape((B, H, N, CHUNK) + t.shape[3:])

    q, k, v, g, beta = chunks(q), chunks(k), chunks(v), chunks(g), chunks(beta)
    gc = jnp.cumsum(g, axis=-1)
    incl = jnp.tril(jnp.ones((CHUNK, CHUNK), dtype=bool))
    strict = jnp.tril(jnp.ones((CHUNK, CHUNK), dtype=bool), -1)
    decay = jnp.where(incl, jnp.exp(jnp.where(incl, gc[..., :, None] - gc[..., None, :], 0.0)), 0.0)
    k_beta = k * beta[..., None]
    v_beta = v * beta[..., None]
    L = jnp.where(strict, jnp.einsum('bhncd,bhnsd->bhncs', k_beta, k) * decay, 0.0)
    eye = jnp.eye(CHUNK, dtype=q.dtype)
    t_inv = lax.linalg.triangular_solve(L + eye, jnp.broadcast_to(eye, L.shape),
                                        left_side=True, lower=True, unit_diagonal=True)
    u = jnp.einsum('bhncs,bhnse->bhnce', t_inv, v_beta)
    w = jnp.einsum('bhncs,bhnsd->bhncd', t_inv, k_beta * jnp.exp(gc)[..., None])
    a_intra = jnp.where(incl, jnp.einsum('bhncd,bhnsd->bhncs', q, k) * decay, 0.0)
    q_dec = q * jnp.exp(gc)[..., None]
    k_dec = k * jnp.exp(gc[..., -1:] - gc)[..., None]
    g_last = jnp.exp(gc[..., -1])
    xs = (jnp.moveaxis(q_dec, 2, 0), jnp.moveaxis(k_dec, 2, 0), jnp.moveaxis(u, 2, 0),
          jnp.moveaxis(w, 2, 0), jnp.moveaxis(a_intra, 2, 0), jnp.moveaxis(g_last, 2, 0))

    def step(S, inp):
        qd, kd, ui, wi, ai, gl = inp
        v_new = ui - jnp.einsum('bhcd,bhde->bhce', wi, S)
        o = jnp.einsum('bhcd,bhde->bhce', qd, S) + jnp.einsum('bhcs,bhse->bhce', ai, v_new)
        S = S * gl[..., None, None] + jnp.einsum('bhcd,bhce->bhde', kd, v_new)
        return S, o

    S0 = jnp.zeros((B, H, DK, DV), q.dtype)
    _, o = lax.scan(step, S0, xs)
    return jnp.moveaxis(o, 0, 2).reshape(B, H, T, DV)


def gdn_group(p, conv_w, a_log, dt_bias, norm_w):
    B, T, _ = p.shape
    qkv = jax.nn.silu(short_conv(p[..., :3 * GDN_W], conv_w)).astype(jnp.float32)
    q, k, v = jnp.split(qkv, 3, axis=-1)

    def heads(t):
        return t.reshape(B, T, GDN_HEADS, HEAD_DIM).transpose(0, 2, 1, 3)

    q = l2norm(heads(q))
    k = l2norm(heads(k))
    v = heads(v)
    z = p[..., 3 * GDN_W:4 * GDN_W].astype(jnp.float32).reshape(B, T, GDN_HEADS, HEAD_DIM)
    off = 4 * GDN_W
    b = p[..., off:off + 2 * GDN_HEADS].astype(jnp.float32).reshape(B, T, 2, GDN_HEADS)
    a = p[..., off + 2 * GDN_HEADS:GDN_IN].astype(jnp.float32).reshape(B, T, 2, GDN_HEADS)
    beta = jnp.transpose(jax.nn.sigmoid(b), (2, 0, 3, 1))
    g = jnp.transpose(-jnp.exp(a_log.astype(jnp.float32))
                      * jax.nn.softplus(a + dt_bias.astype(jnp.float32)), (2, 0, 3, 1))

    def flip(t):
        return jnp.flip(t, axis=2)

    o_fwd = gated_delta_chunked(q, k, v, g[0], beta[0])
    o_bwd = flip(gated_delta_chunked(flip(q), flip(k), flip(v), flip(g[1]), flip(beta[1])))
    o = (o_fwd + o_bwd).transpose(0, 2, 1, 3)
    y = rmsnorm(o, norm_w) * jax.nn.silu(z)
    return y.reshape(B, T, GDN_W)


def axial_rope_tables(T):
    rows = T // GRID_W
    row = jnp.repeat(jnp.arange(rows, dtype=jnp.float32), GRID_W)
    col = jnp.tile(jnp.arange(GRID_W, dtype=jnp.float32), rows)
    inv_freq = ROPE_THETA ** (-jnp.arange(0, ROT_HALF, 2, dtype=jnp.float32) / ROT_HALF)
    ang_r = row[:, None] * inv_freq[None, :]
    ang_c = col[:, None] * inv_freq[None, :]
    ang = jnp.concatenate([ang_r, ang_r, ang_c, ang_c], axis=-1)[:, None, :]
    return jnp.cos(ang), jnp.sin(ang)


def rotate_half_axial(x):
    x1, x2, x3, x4 = jnp.split(x, 4, axis=-1)
    return jnp.concatenate([-x2, x1, -x4, x3], axis=-1)


def apply_axial_rope(x, cos, sin):
    return x * cos + rotate_half_axial(x) * sin


def attn_group(p, q_norm_w, k_norm_w, cos, sin):
    B, T, _ = p.shape
    q = p[..., :ATT_W].reshape(B, T, ATT_HEADS, HEAD_DIM)
    k = p[..., ATT_W:ATT_W + KV_W].reshape(B, T, ATT_KV_HEADS, HEAD_DIM)
    v = p[..., ATT_W + KV_W:].reshape(B, T, ATT_KV_HEADS, HEAD_DIM)
    q = apply_axial_rope(rmsnorm(q, q_norm_w), cos, sin).astype(p.dtype)
    k = apply_axial_rope(rmsnorm(k, k_norm_w), cos, sin).astype(p.dtype)
    nb = T // Q_BLOCK
    qb = q.reshape(B, nb, Q_BLOCK, ATT_KV_HEADS, ATT_GROUPS, HEAD_DIM).transpose(1, 0, 3, 4, 2, 5)
    kt = k.transpose(0, 2, 1, 3)
    vt = v.transpose(0, 2, 1, 3)
    scale = HEAD_DIM ** -0.5

    def block(qi):
        s = jnp.einsum('bhgqd,bhkd->bhgqk', qi, kt, preferred_element_type=jnp.float32) * scale
        pr = jax.nn.softmax(s, axis=-1).astype(vt.dtype)
        return jnp.einsum('bhgqk,bhkd->bhgqd', pr, vt)

    o = lax.map(block, qb)
    return o.transpose(1, 0, 4, 2, 3, 5).reshape(B, T, ATT_W)


def encoder_trunk(x, c, ada_w, ada_b, norm1_w, norm2_w, w_in, conv_w, a_log, dt_bias,
                  gdn_norm_w, q_norm_w, k_norm_w, w_out, w_up, w_down):
    B, T, _ = x.shape
    cos, sin = axial_rope_tables(T)
    for l in range(DEPTH):
        mod = jax.nn.silu(c) @ ada_w[l] + ada_b[l]
        sh1, sc1, gt1, sh2, sc2, gt2 = [m[:, None, :] for m in jnp.split(mod, N_MOD, axis=-1)]
        h = (rmsnorm(x, norm1_w[l]) * (1.0 + sc1) + sh1).astype(x.dtype)
        p = h @ w_in[l]
        ya = gdn_group(p[..., :GDN_IN], conv_w[l], a_log[l], dt_bias[l], gdn_norm_w[l])
        yb = attn_group(p[..., GDN_IN:], q_norm_w[l], k_norm_w[l], cos, sin)
        mix = jnp.concatenate([ya.astype(x.dtype), yb.astype(x.dtype)], axis=-1) @ w_out[l]
        x = x + gt1 * mix
        h2 = (rmsnorm(x, norm2_w[l]) * (1.0 + sc2) + sh2).astype(x.dtype)
        x = x + gt2 * (jnp.square(jax.nn.relu(h2 @ w_up[l])) @ w_down[l])
    return x


def setup_inputs(seed: int = 0) -> dict:
    key = jax.random.key(seed)
    ks = jax.random.split(key, 18)

    def nrm(k, shape, s):
        return jax.random.normal(k, shape, jnp.float32) * s

    dt = jnp.exp(jax.random.uniform(ks[11], (DEPTH, 2, GDN_HEADS), jnp.float32,
                                    math.log(1e-3), math.log(1e-1)))
    return {
        'x_prompt': nrm(ks[0], (BATCH, SEQ, D_MODEL), 1.0),
        'x_sample': nrm(ks[1], (DEC_BATCH, DEC_SEQ, D_MODEL), 1.0),
        'c_prompt': nrm(ks[2], (BATCH, D_MODEL), 1.0),
        'c_sample': nrm(ks[3], (DEC_BATCH, D_MODEL), 1.0),
        'ada_w': nrm(ks[4], (DEPTH, D_MODEL, N_MOD * D_MODEL), 0.5 * D_MODEL ** -0.5),
        'ada_b': nrm(ks[5], (DEPTH, N_MOD * D_MODEL), 0.02),
        'norm1_w': 1.0 + nrm(ks[6], (DEPTH, D_MODEL), 0.02),
        'norm2_w': 1.0 + nrm(ks[7], (DEPTH, D_MODEL), 0.02),
        'w_in': nrm(ks[8], (DEPTH, D_MODEL, IN_COLS), D_MODEL ** -0.5),
        'conv_w': nrm(ks[9], (DEPTH, CONV_K, 3 * GDN_W), CONV_K ** -0.5),
        'a_log': jnp.log(jax.random.uniform(ks[10], (DEPTH, 2, GDN_HEADS), jnp.float32, 1.0, 16.0)),
        'dt_bias': dt + jnp.log(-jnp.expm1(-dt)),
        'gdn_norm_w': 1.0 + nrm(ks[12], (DEPTH, HEAD_DIM), 0.02),
        'q_norm_w': 1.0 + nrm(ks[13], (DEPTH, HEAD_DIM), 0.02),
        'k_norm_w': 1.0 + nrm(ks[14], (DEPTH, HEAD_DIM), 0.02),
        'w_out': nrm(ks[15], (DEPTH, D_MIX, D_MODEL), D_MIX ** -0.5),
        'w_up': nrm(ks[16], (DEPTH, D_MODEL, D_FF), D_MODEL ** -0.5),
        'w_down': nrm(ks[17], (DEPTH, D_FF, D_MODEL), D_FF ** -0.5),
    }


def reference(x_prompt, x_sample, c_prompt, c_sample, ada_w, ada_b, norm1_w, norm2_w, w_in,
              conv_w, a_log, dt_bias, gdn_norm_w, q_norm_w, k_norm_w, w_out, w_up, w_down):
    y_prompt = encoder_trunk(x_prompt, c_prompt, ada_w, ada_b, norm1_w, norm2_w, w_in, conv_w,
                             a_log, dt_bias, gdn_norm_w, q_norm_w, k_norm_w, w_out, w_up, w_down)
    y_sample = encoder_trunk(x_sample, c_sample, ada_w, ada_b, norm1_w, norm2_w, w_in, conv_w,
                             a_log, dt_bias, gdn_norm_w, q_norm_w, k_norm_w, w_out, w_up, w_down)
    return (y_prompt, y_sample)
```

```python
import functools

import jax
import jax.numpy as jnp
from jax import lax
from jax.experimental import pallas as pl
from jax.experimental.pallas import tpu as pltpu

F32 = jnp.float32
BF16 = jnp.bfloat16

HEAD_DIM = 128
GDN_HEADS = 8
ATT_HEADS = 8
ATT_KV_HEADS = 2
ATT_GROUPS = ATT_HEADS // ATT_KV_HEADS
CONV_K = 5
CHUNK = 64
PAIR = 2 * CHUNK
GRID_W = 64
ROPE_THETA = 10000.0
ROT_HALF = HEAD_DIM // 2
N_MOD = 6
EPS = 1e-6
NEUMANN_STEPS = 5

VMEM_LIMIT = 56 * 1024 * 1024


def _sigmoid(x):
    return 1.0 / (1.0 + jnp.exp(-x))


def _silu(x):
    return x * _sigmoid(x)


def _softplus(x):
    return jnp.maximum(x, 0.0) + jnp.log(1.0 + jnp.exp(-jnp.abs(x)))


def _dot(a, b):
    return jnp.dot(a, b, preferred_element_type=F32)


def _dot_nt(a, b):
    return lax.dot_general(a, b, (((1,), (1,)), ((), ())), preferred_element_type=F32)


def _params(sem):
    return pltpu.CompilerParams(dimension_semantics=sem, vmem_limit_bytes=VMEM_LIMIT)


def _adaln_kernel(c_ref, w_ref, b_ref, o_ref):
    s = _silu(c_ref[...]).astype(BF16)
    o_ref[...] = _dot(s, w_ref[...].astype(BF16)) + b_ref[...]


def _adaln(c, ada_w, ada_b, tn=1024):
    depth, d, n = ada_w.shape
    rows = c.shape[0]
    return pl.pallas_call(
        _adaln_kernel,
        grid=(depth, n // tn),
        in_specs=[
            pl.BlockSpec((rows, d), lambda l, j: (0, 0)),
            pl.BlockSpec((None, d, tn), lambda l, j: (l, 0, j)),
            pl.BlockSpec((None, 1, tn), lambda l, j: (l, 0, j)),
        ],
        out_specs=pl.BlockSpec((None, rows, tn), lambda l, j: (l, 0, j)),
        out_shape=jax.ShapeDtypeStruct((depth, rows, n), F32),
        compiler_params=_params(("arbitrary", "arbitrary")),
    )(c, ada_w, ada_b.reshape(depth, 1, n))


def _modulated_norm(x, nw, sc, sh):
    y = x * lax.rsqrt(jnp.mean(x * x, axis=-1, keepdims=True) + EPS) * nw
    return y * (1.0 + sc) + sh


def _in_proj_kernel(x_ref, nw_ref, sc_ref, sh_ref, w_ref, wba_ref, o_ref, ba_ref, h_ref):
    @pl.when(pl.program_id(2) == 0)
    def _():
        h = _modulated_norm(x_ref[...], nw_ref[...], sc_ref[...], sh_ref[...]).astype(BF16)
        h_ref[...] = h
        ba_ref[...] = _dot(h, wba_ref[...])

    o_ref[...] = _dot(h_ref[...], w_ref[...]).astype(o_ref.dtype)


def _in_proj(x, nw, sc, sh, w_main, w_ba, tm=512, tn=1408):
    b, t, d = x.shape
    n = w_main.shape[1]
    nba = w_ba.shape[1]
    return pl.pallas_call(
        _in_proj_kernel,
        grid=(b, t // tm, n // tn),
        in_specs=[
            pl.BlockSpec((None, tm, d), lambda bi, i, j: (bi, i, 0)),
            pl.BlockSpec((1, d), lambda bi, i, j: (0, 0)),
            pl.BlockSpec((None, 1, d), lambda bi, i, j: (bi, 0, 0)),
            pl.BlockSpec((None, 1, d), lambda bi, i, j: (bi, 0, 0)),
            pl.BlockSpec((d, tn), lambda bi, i, j: (0, j)),
            pl.BlockSpec((d, nba), lambda bi, i, j: (0, 0)),
        ],
        out_specs=[
            pl.BlockSpec((None, tm, tn), lambda bi, i, j: (bi, i, j)),
            pl.BlockSpec((None, tm, nba), lambda bi, i, j: (bi, i, 0)),
        ],
        out_shape=[
            jax.ShapeDtypeStruct((b, t, n), BF16),
            jax.ShapeDtypeStruct((b, t, nba), F32),
        ],
        scratch_shapes=[pltpu.VMEM((tm, d), BF16)],
        compiler_params=_params(("arbitrary", "arbitrary", "arbitrary")),
    )(x, nw, sc, sh, w_main, w_ba)


CONV_TILE = 256
LANE_BETA = (0, 8)
LANE_G = (16, 24)


def _gdn_kernel(q_ref, k_ref, v_ref, z_ref, ba_ref, cwq_ref, cwk_ref, cwv_ref, alog_ref, dt_ref, nw_ref,
                o_ref,
                pad_ref, qs_ref, ks_ref, vs_ref, gsel_ref, gc_ref, grow_ref,
                u_ref, wq_ref, kdt_ref, a_ref, gl_ref, s_ref, oacc_ref):
    t = q_ref.shape[0]
    n_tiles = t // CONV_TILE
    n_pairs = t // PAIR
    n_chunks = t // CHUNK
    head = pl.program_id(1)

    halo = 8
    pad_ref[0:halo, :] = jnp.zeros((halo, HEAD_DIM), F32)
    pad_ref[halo + t:halo + t + halo, :] = jnp.zeros((halo, HEAD_DIM), F32)

    def conv_silu(src_ref, cw_ref, dst_ref, normalize, scale):
        def fill(i, c):
            r = pl.multiple_of(i * CONV_TILE, CONV_TILE)
            pad_ref[pl.ds(halo + r, CONV_TILE), :] = src_ref[pl.ds(r, CONV_TILE), :].astype(F32)
            return c

        lax.fori_loop(0, n_tiles, fill, 0)
        cw = cw_ref[...]

        def body(i, c):
            r = pl.multiple_of(i * CONV_TILE, CONV_TILE)
            win = pad_ref[pl.ds(r, CONV_TILE + 2 * halo), :]
            acc = None
            for j in range(CONV_K):
                off = halo - CONV_K // 2 + j
                term = win[off:off + CONV_TILE, :] * cw[j:j + 1, :]
                acc = term if acc is None else acc + term
            y = _silu(acc)
            if normalize:
                y = y * lax.rsqrt(jnp.sum(y * y, axis=-1, keepdims=True) + EPS)
            if scale != 1.0:
                y = y * scale
            dst_ref[pl.ds(r, CONV_TILE), :] = y
            return c

        lax.fori_loop(0, n_tiles, body, 0)

    conv_silu(q_ref, cwq_ref, qs_ref, True, HEAD_DIM ** -0.5)
    conv_silu(k_ref, cwk_ref, ks_ref, True, 1.0)
    conv_silu(v_ref, cwv_ref, vs_ref, False, 1.0)

    neg_a = -jnp.exp(alog_ref[...])
    dt_row = dt_ref[...]

    def gate_body(i, c):
        r = pl.multiple_of(i * CONV_TILE, CONV_TILE)
        ba = ba_ref[pl.ds(r, CONV_TILE), :]
        lane = lax.broadcasted_iota(jnp.int32, ba.shape, 1)
        comb = jnp.where(lane < 2 * GDN_HEADS, _sigmoid(ba), neg_a * _softplus(ba + dt_row))
        for bit in (1, 2, 4):
            rolled = pltpu.roll(comb, HEAD_DIM - bit, 1)
            comb = jnp.where((head & bit) != 0, rolled, comb)
        gsel_ref[pl.ds(r, CONV_TILE), :] = comb
        return c

    lax.fori_loop(0, n_tiles, gate_body, 0)

    ri = lax.broadcasted_iota(jnp.int32, (PAIR, PAIR), 0)
    ci = lax.broadcasted_iota(jnp.int32, (PAIR, PAIR), 1)
    same = (ri // CHUNK) == (ci // CHUNK)
    m_low = jnp.where(same & (ci <= ri), 1.0, 0.0).astype(BF16)
    m_up = jnp.where(same & (ci >= ri), 1.0, 0.0).astype(BF16)

    def cum_body(i, c):
        r = pl.multiple_of(i * PAIR, PAIR)
        x = gsel_ref[pl.ds(r, PAIR), :]
        hi = x.astype(BF16)
        r1 = x - hi.astype(F32)
        mid = r1.astype(BF16)
        lo = (r1 - mid.astype(F32)).astype(BF16)
        pre = _dot(m_low, hi) + _dot(m_low, mid) + _dot(m_low, lo)
        suf = _dot(m_up, hi) + _dot(m_up, mid) + _dot(m_up, lo)
        lane = lax.broadcasted_iota(jnp.int32, x.shape, 1)
        gc = jnp.where(lane == LANE_G[1], suf, pre)
        gc_ref[pl.ds(r, PAIR), :] = gc
        gct = gc.T
        grow_ref[i, 0:1, :] = gct[LANE_G[0]:LANE_G[0] + 1, :]
        grow_ref[i, 1:2, :] = gct[LANE_G[1]:LANE_G[1] + 1, :]
        return c

    lax.fori_loop(0, n_pairs, cum_body, 0)

    r64 = lax.broadcasted_iota(jnp.int32, (CHUNK, CHUNK), 0)
    c64 = lax.broadcasted_iota(jnp.int32, (CHUNK, CHUNK), 1)
    eye = jnp.where(r64 == c64, 1.0, 0.0)
    zeros_half = jnp.zeros((CHUNK, HEAD_DIM), F32)

    def chunk_local(d, i, half):
        r0 = pl.multiple_of(i * PAIR + half * CHUNK, CHUNK)
        q = qs_ref[pl.ds(r0, CHUNK), :]
        k = ks_ref[pl.ds(r0, CHUNK), :]
        v = vs_ref[pl.ds(r0, CHUNK), :]
        gs = gsel_ref[pl.ds(r0, CHUNK), :]
        gcv = gc_ref[pl.ds(r0, CHUNK), :]
        beta = gs[:, LANE_BETA[d]:LANE_BETA[d] + 1]
        gcol = gcv[:, LANE_G[d]:LANE_G[d] + 1]
        grow = grow_ref[i][d:d + 1, half * CHUNK:(half + 1) * CHUNK]
        last = CHUNK - 1 if d == 0 else 0
        glast = gcol[last:last + 1, :]
        eg = jnp.exp(gcol)
        kb = k * beta
        vb = v * beta
        kbe = kb * eg
        qd = q * eg
        kd = k * jnp.exp(glast - gcol)
        incl = (r64 >= c64) if d == 0 else (r64 <= c64)
        strict = (r64 > c64) if d == 0 else (r64 < c64)
        decay = jnp.where(incl, jnp.exp(jnp.where(incl, gcol - grow, 0.0)), 0.0)
        kq = _dot_nt(jnp.concatenate([kb, q], axis=0).astype(BF16), k.astype(BF16))
        lmat = jnp.where(strict, kq[:CHUNK] * decay, 0.0)
        amat = jnp.where(incl, kq[CHUNK:] * decay, 0.0)
        rinv = eye - lmat
        p = lmat
        for _ in range(NEUMANN_STEPS):
            pb = p.astype(BF16)
            p = _dot(pb, pb)
            rinv = rinv + _dot(rinv.astype(BF16), p.astype(BF16))
        uw = _dot(rinv.astype(BF16), jnp.concatenate([vb, kbe], axis=1).astype(BF16))
        u_ref[d, pl.ds(r0, CHUNK), :] = uw[:, :HEAD_DIM]
        wq_ref[d, 2 * i + half] = jnp.concatenate([uw[:, HEAD_DIM:], qd], axis=0).astype(BF16)
        a_ref[d, pl.ds(r0, CHUNK), :] = amat.astype(BF16)
        gl_ref[d, 2 * i + half] = jnp.broadcast_to(jnp.exp(glast), (8, HEAD_DIM))
        return kd

    def local_body(i, c):
        for d in range(2):
            kd0 = chunk_local(d, i, 0)
            kd1 = chunk_local(d, i, 1)
            kdt_ref[d, i] = jnp.concatenate([kd0, kd1], axis=0).T.astype(BF16)
        return c

    lax.fori_loop(0, n_pairs, local_body, 0)

    s_ref[...] = jnp.zeros(s_ref.shape, F32)

    def oacc_zero(i, c):
        r = pl.multiple_of(i * CONV_TILE, CONV_TILE)
        oacc_ref[pl.ds(r, CONV_TILE), :] = jnp.zeros((CONV_TILE, HEAD_DIM), F32)
        return c

    lax.fori_loop(0, n_tiles, oacc_zero, 0)

    def sweep_step(d, c):
        r0 = pl.multiple_of(c * CHUNK, CHUNK)
        half = c % 2
        s = s_ref[d]
        wqs = _dot(wq_ref[d, c], s.astype(BF16))
        vnew = (u_ref[d, pl.ds(r0, CHUNK), :] - wqs[:CHUNK]).astype(BF16)
        o = wqs[CHUNK:] + _dot(a_ref[d, pl.ds(r0, CHUNK), :], vnew)
        zpad = jnp.zeros((CHUNK, HEAD_DIM), BF16)
        vpad = jnp.where(half == 0,
                         jnp.concatenate([vnew, zpad], axis=0),
                         jnp.concatenate([zpad, vnew], axis=0))
        gl = gl_ref[d, c][0:1, :]
        s_ref[d] = s * gl + _dot(kdt_ref[d, c // 2], vpad)
        oacc_ref[pl.ds(r0, CHUNK), :] += o

    def sweep_body(it, c):
        sweep_step(0, it)
        sweep_step(1, n_chunks - 1 - it)
        return c

    lax.fori_loop(0, n_chunks, sweep_body, 0)

    nw = nw_ref[...]

    def out_body(i, c):
        r = pl.multiple_of(i * CONV_TILE, CONV_TILE)
        o = oacc_ref[pl.ds(r, CONV_TILE), :]
        y = o * lax.rsqrt(jnp.mean(o * o, axis=-1, keepdims=True) + EPS) * nw
        z = z_ref[pl.ds(r, CONV_TILE), :].astype(F32)
        o_ref[pl.ds(r, CONV_TILE), :] = (y * _silu(z)).astype(o_ref.dtype)
        return c

    lax.fori_loop(0, n_tiles, out_body, 0)


def _gdn(p_main, ba, conv_w, alog_row, dt_row, norm_w):
    b, t, _ = p_main.shape
    h = GDN_HEADS
    n_chunks = t // CHUNK
    n_pairs = t // PAIR
    col = lambda off: (lambda bi, hi: (bi, 0, off + hi))
    cwcol = lambda off: (lambda bi, hi: (0, off + hi))
    const2 = lambda bi, hi: (0, 0)
    return pl.pallas_call(
        _gdn_kernel,
        grid=(b, h),
        in_specs=[
            pl.BlockSpec((None, t, HEAD_DIM), col(0)),
            pl.BlockSpec((None, t, HEAD_DIM), col(h)),
            pl.BlockSpec((None, t, HEAD_DIM), col(2 * h)),
            pl.BlockSpec((None, t, HEAD_DIM), col(3 * h)),
            pl.BlockSpec((None, t, HEAD_DIM), lambda bi, hi: (bi, 0, 0)),
            pl.BlockSpec((CONV_K, HEAD_DIM), cwcol(0)),
            pl.BlockSpec((CONV_K, HEAD_DIM), cwcol(h)),
            pl.BlockSpec((CONV_K, HEAD_DIM), cwcol(2 * h)),
            pl.BlockSpec((1, HEAD_DIM), const2),
            pl.BlockSpec((1, HEAD_DIM), const2),
            pl.BlockSpec((1, HEAD_DIM), const2),
        ],
        out_specs=pl.BlockSpec((None, t, HEAD_DIM), lambda bi, hi: (bi, 0, hi)),
        out_shape=jax.ShapeDtypeStruct((b, t, h * HEAD_DIM), BF16),
        scratch_shapes=[
            pltpu.VMEM((t + 16, HEAD_DIM), F32),
            pltpu.VMEM((t, HEAD_DIM), F32),
            pltpu.VMEM((t, HEAD_DIM), F32),
            pltpu.VMEM((t, HEAD_DIM), F32),
            pltpu.VMEM((t, HEAD_DIM), F32),
            pltpu.VMEM((t, HEAD_DIM), F32),
            pltpu.VMEM((n_pairs, 8, HEAD_DIM), F32),
            pltpu.VMEM((2, t, HEAD_DIM), F32),
            pltpu.VMEM((2, n_chunks, 2 * CHUNK, HEAD_DIM), BF16),
            pltpu.VMEM((2, n_pairs, HEAD_DIM, PAIR), BF16),
            pltpu.VMEM((2, t, CHUNK), BF16),
            pltpu.VMEM((2, n_chunks, 8, HEAD_DIM), F32),
            pltpu.VMEM((2, HEAD_DIM, HEAD_DIM), F32),
            pltpu.VMEM((t, HEAD_DIM), F32),
        ],
        compiler_params=_params(("arbitrary", "arbitrary")),
    )(p_main, p_main, p_main, p_main, ba, conv_w, conv_w, conv_w, alog_row, dt_row, norm_w)


def _attn_prep_kernel(q_ref, kv_ref, qw_ref, kw_ref, cos_ref, sa_ref, sb_ref, qo_ref, ko_ref):
    cos = cos_ref[...]
    sa = sa_ref[...]
    sb = sb_ref[...]

    def norm_rope(x, w):
        y = x * lax.rsqrt(jnp.mean(x * x, axis=-1, keepdims=True) + EPS) * w
        return y * cos + pltpu.roll(y, HEAD_DIM - ROT_HALF // 2, 1) * sa + pltpu.roll(y, ROT_HALF // 2, 1) * sb

    qw = qw_ref[...]
    kw = kw_ref[...]
    scale = HEAD_DIM ** -0.5
    for hh in range(ATT_HEADS):
        sl = slice(hh * HEAD_DIM, (hh + 1) * HEAD_DIM)
        qo_ref[:, sl] = (norm_rope(q_ref[:, sl].astype(F32), qw) * scale).astype(qo_ref.dtype)
    for hh in range(ATT_KV_HEADS):
        sl = slice(hh * HEAD_DIM, (hh + 1) * HEAD_DIM)
        ko_ref[:, sl] = norm_rope(kv_ref[:, sl].astype(F32), kw).astype(ko_ref.dtype)


def _attn_prep(p_main, q_off, qw, kw, cos, sa, sb, tq=256):
    b, t, _ = p_main.shape
    qwid = ATT_HEADS * HEAD_DIM
    kvwid = 2 * ATT_KV_HEADS * HEAD_DIM
    row = lambda bi, i: (i, 0)
    const2 = lambda bi, i: (0, 0)
    return pl.pallas_call(
        _attn_prep_kernel,
        grid=(b, t // tq),
        in_specs=[
            pl.BlockSpec((None, tq, qwid), lambda bi, i: (bi, i, q_off // qwid)),
            pl.BlockSpec((None, tq, kvwid), lambda bi, i: (bi, i, (q_off + qwid) // kvwid)),
            pl.BlockSpec((1, HEAD_DIM), const2),
            pl.BlockSpec((1, HEAD_DIM), const2),
            pl.BlockSpec((tq, HEAD_DIM), row),
            pl.BlockSpec((tq, HEAD_DIM), row),
            pl.BlockSpec((tq, HEAD_DIM), row),
        ],
        out_specs=[
            pl.BlockSpec((None, tq, qwid), lambda bi, i: (bi, i, 0)),
            pl.BlockSpec((None, tq, ATT_KV_HEADS * HEAD_DIM), lambda bi, i: (bi, i, 0)),
        ],
        out_shape=[
            jax.ShapeDtypeStruct((b, t, qwid), BF16),
            jax.ShapeDtypeStruct((b, t, ATT_KV_HEADS * HEAD_DIM), BF16),
        ],
        compiler_params=_params(("arbitrary", "arbitrary")),
    )(p_main, p_main, qw, kw, cos, sa, sb)


def _attn_kernel(q_ref, k_ref, v_ref, o_ref, m_ref, l_ref, acc_ref):
    j = pl.program_id(3)
    tq = q_ref.shape[0]

    @pl.when(j == 0)
    def _():
        m_ref[...] = jnp.full(m_ref.shape, -jnp.inf, F32)
        l_ref[...] = jnp.zeros(l_ref.shape, F32)
        acc_ref[...] = jnp.zeros(acc_ref.shape, F32)

    q = jnp.concatenate([q_ref[:, g * HEAD_DIM:(g + 1) * HEAD_DIM] for g in range(ATT_GROUPS)], axis=0)
    s = _dot_nt(q, k_ref[...])
    m_prev = m_ref[...]
    m_new = jnp.maximum(m_prev, jnp.max(s, axis=-1, keepdims=True))
    alpha = jnp.exp(m_prev - m_new)
    p = jnp.exp(s - m_new[:, 0:1])
    l_ref[...] = alpha * l_ref[...] + jnp.sum(p, axis=-1, keepdims=True)
    acc_ref[...] = alpha * acc_ref[...] + _dot(p.astype(BF16), v_ref[...])
    m_ref[...] = m_new

    @pl.when(j == pl.num_programs(3) - 1)
    def _():
        o = acc_ref[...] / l_ref[...]
        for g in range(ATT_GROUPS):
            o_ref[:, g * HEAD_DIM:(g + 1) * HEAD_DIM] = o[g * tq:(g + 1) * tq, :].astype(o_ref.dtype)


def _attention(q_rot, k_rot, p_main, v_off, tq=128, tk=1024):
    b, t, _ = q_rot.shape
    tk = min(tk, t)
    gw = ATT_GROUPS * HEAD_DIM
    vblk = v_off // HEAD_DIM
    return pl.pallas_call(
        _attn_kernel,
        grid=(b, ATT_KV_HEADS, t // tq, t // tk),
        in_specs=[
            pl.BlockSpec((None, tq, gw), lambda bi, hi, i, j: (bi, i, hi)),
            pl.BlockSpec((None, tk, HEAD_DIM), lambda bi, hi, i, j: (bi, j, hi)),
            pl.BlockSpec((None, tk, HEAD_DIM), lambda bi, hi, i, j: (bi, j, vblk + hi)),
        ],
        out_specs=pl.BlockSpec((None, tq, gw), lambda bi, hi, i, j: (bi, i, hi)),
        out_shape=jax.ShapeDtypeStruct((b, t, ATT_HEADS * HEAD_DIM), BF16),
        scratch_shapes=[
            pltpu.VMEM((ATT_GROUPS * tq, HEAD_DIM), F32),
            pltpu.VMEM((ATT_GROUPS * tq, HEAD_DIM), F32),
            pltpu.VMEM((ATT_GROUPS * tq, HEAD_DIM), F32),
        ],
        compiler_params=_params(("arbitrary", "arbitrary", "arbitrary", "arbitrary")),
    )(q_rot, k_rot, p_main)


def _out_proj_kernel(ya_ref, yb_ref, wa_ref, wb_ref, x_ref, gt_ref, o_ref):
    mix = _dot(ya_ref[...], wa_ref[...]) + _dot(yb_ref[...], wb_ref[...])
    o_ref[...] = x_ref[...] + gt_ref[...] * mix


def _out_proj(ya, yb, w_out, x, gt, tm=512):
    b, t, d = x.shape
    ka = ya.shape[2]
    kb = yb.shape[2]
    return pl.pallas_call(
        _out_proj_kernel,
        grid=(b, t // tm),
        in_specs=[
            pl.BlockSpec((None, tm, ka), lambda bi, i: (bi, i, 0)),
            pl.BlockSpec((None, tm, kb), lambda bi, i: (bi, i, 0)),
            pl.BlockSpec((ka, d), lambda bi, i: (0, 0)),
            pl.BlockSpec((kb, d), lambda bi, i: (ka // kb, 0)),
            pl.BlockSpec((None, tm, d), lambda bi, i: (bi, i, 0)),
            pl.BlockSpec((None, 1, d), lambda bi, i: (bi, 0, 0)),
        ],
        out_specs=pl.BlockSpec((None, tm, d), lambda bi, i: (bi, i, 0)),
        out_shape=jax.ShapeDtypeStruct((b, t, d), F32),
        compiler_params=_params(("arbitrary", "arbitrary")),
    )(ya, yb, w_out, w_out, x, gt)


def _mlp_kernel(x_ref, nw_ref, sc_ref, sh_ref, gt_ref, wu_ref, wd_ref, o_ref, h_ref, acc_ref):
    f = pl.program_id(2)

    @pl.when(f == 0)
    def _():
        h_ref[...] = _modulated_norm(x_ref[...], nw_ref[...], sc_ref[...], sh_ref[...]).astype(BF16)
        acc_ref[...] = jnp.zeros(acc_ref.shape, F32)

    u = jnp.maximum(_dot(h_ref[...], wu_ref[...]), 0.0)
    acc_ref[...] += _dot((u * u).astype(BF16), wd_ref[...])

    @pl.when(f == pl.num_programs(2) - 1)
    def _():
        o_ref[...] = x_ref[...] + gt_ref[...] * acc_ref[...]


def _mlp(x, nw, sc, sh, gt, w_up, w_down, tm=512, tf=512):
    b, t, d = x.shape
    dff = w_up.shape[1]
    vec = lambda bi, i, f: (bi, 0, 0)
    return pl.pallas_call(
        _mlp_kernel,
        grid=(b, t // tm, dff // tf),
        in_specs=[
            pl.BlockSpec((None, tm, d), lambda bi, i, f: (bi, i, 0)),
            pl.BlockSpec((1, d), lambda bi, i, f: (0, 0)),
            pl.BlockSpec((None, 1, d), vec),
            pl.BlockSpec((None, 1, d), vec),
            pl.BlockSpec((None, 1, d), vec),
            pl.BlockSpec((d, tf), lambda bi, i, f: (0, f)),
            pl.BlockSpec((tf, d), lambda bi, i, f: (f, 0)),
        ],
        out_specs=pl.BlockSpec((None, tm, d), lambda bi, i, f: (bi, i, 0)),
        out_shape=jax.ShapeDtypeStruct((b, t, d), F32),
        scratch_shapes=[pltpu.VMEM((tm, d), BF16), pltpu.VMEM((tm, d), F32)],
        compiler_params=_params(("arbitrary", "arbitrary", "arbitrary")),
    )(x, nw, sc, sh, gt, w_up, w_down)


def _rope_tables(t):
    rows = t // GRID_W
    row = jnp.repeat(jnp.arange(rows, dtype=F32), GRID_W)
    col = jnp.tile(jnp.arange(GRID_W, dtype=F32), rows)
    inv_freq = ROPE_THETA ** (-jnp.arange(0, ROT_HALF, 2, dtype=F32) / ROT_HALF)
    ang_r = row[:, None] * inv_freq[None, :]
    ang_c = col[:, None] * inv_freq[None, :]
    ang = jnp.concatenate([ang_r, ang_r, ang_c, ang_c], axis=-1)
    cos, sin = jnp.cos(ang), jnp.sin(ang)
    first = (jnp.arange(HEAD_DIM) % ROT_HALF) < ROT_HALF // 2
    return cos, jnp.where(first, -sin, 0.0), jnp.where(first, 0.0, sin)


def _gate_rows(a_log, dt_bias):
    def place(p):
        return jnp.pad(p.reshape(1, 2 * GDN_HEADS), ((0, 0), (2 * GDN_HEADS, HEAD_DIM - 4 * GDN_HEADS)))
    return place(a_log.astype(F32)), place(dt_bias.astype(F32))


def _trunk_layer(x, mod, norm1_w, norm2_w, w_main, w_ba, conv_w, alog_row, dt_row, gdn_norm_w,
                 q_norm_w, k_norm_w, w_out, w_up, w_down, rope):
    b, t, d = x.shape
    sh1, sc1, gt1, sh2, sc2, gt2 = [m.reshape(b, 1, d) for m in jnp.split(mod, N_MOD, axis=-1)]
    gdn_w = GDN_HEADS * HEAD_DIM
    p_main, ba = _in_proj(x, norm1_w, sc1, sh1, w_main, w_ba)
    ya = _gdn(p_main, ba, conv_w, alog_row, dt_row, gdn_norm_w)
    att_off = 4 * gdn_w
    q_rot, k_rot = _attn_prep(p_main, att_off, q_norm_w, k_norm_w, *rope)
    v_off = att_off + (ATT_HEADS + ATT_KV_HEADS) * HEAD_DIM
    yb = _attention(q_rot, k_rot, p_main, v_off)
    x = _out_proj(ya, yb, w_out, x, gt1)
    return _mlp(x, norm2_w, sc2, sh2, gt2, w_up, w_down)


def kernel(x_prompt, x_sample, c_prompt, c_sample, ada_w, ada_b, norm1_w, norm2_w, w_in, conv_w, a_log,
           dt_bias, gdn_norm_w, q_norm_w, k_norm_w, w_out, w_up, w_down):
    depth = ada_w.shape[0]
    gdn_w = GDN_HEADS * HEAD_DIM
    n_gate = 4 * GDN_HEADS
    nb_prompt = c_prompt.shape[0]

    mod = _adaln(jnp.concatenate([c_prompt, c_sample], axis=0), ada_w, ada_b)

    w_main = jnp.concatenate([w_in[:, :, :4 * gdn_w], w_in[:, :, 4 * gdn_w + n_gate:]], axis=-1).astype(BF16)
    w_ba = jnp.pad(w_in[:, :, 4 * gdn_w:4 * gdn_w + n_gate], ((0, 0), (0, 0), (0, HEAD_DIM - n_gate))).astype(BF16)
    w_out_b = w_out.astype(BF16)
    w_up_b = w_up.astype(BF16)
    w_down_b = w_down.astype(BF16)

    xs = [x_prompt, x_sample]
    ropes = [_rope_tables(x.shape[1]) for x in xs]
    for l in range(depth):
        alog_row, dt_row = _gate_rows(a_log[l], dt_bias[l])
        mods = [mod[l, :nb_prompt], mod[l, nb_prompt:]]
        for i in range(2):
            xs[i] = _trunk_layer(
                xs[i], mods[i], norm1_w[l].reshape(1, -1), norm2_w[l].reshape(1, -1), w_main[l], w_ba[l],
                conv_w[l], alog_row, dt_row, gdn_norm_w[l].reshape(1, -1), q_norm_w[l].reshape(1, -1),
                k_norm_w[l].reshape(1, -1), w_out_b[l], w_up_b[l], w_down_b[l], ropes[i])
    return (xs[0], xs[1])
```

```python
import functools

import jax
import jax.numpy as jnp
from jax import lax
from jax.experimental import pallas as pl
from jax.experimental.pallas import tpu as pltpu

F32 = jnp.float32
BF16 = jnp.bfloat16

HEAD_DIM = 128
GDN_HEADS = 8
ATT_HEADS = 8
ATT_KV_HEADS = 2
ATT_GROUPS = ATT_HEADS // ATT_KV_HEADS
CONV_K = 5
CHUNK = 64
PAIR = 2 * CHUNK
WIN = 4 * CHUNK
LHS_ROWS = HEAD_DIM + CHUNK
GRID_W = 64
ROPE_THETA = 10000.0
ROT_HALF = HEAD_DIM // 2
N_MOD = 6
EPS = 1e-6
BASE_BLOCK = 4
WINDOWS_PER_STEP = 2

VMEM_LIMIT = 56 * 1024 * 1024


def _sigmoid(x):
    return 1.0 / (1.0 + jnp.exp(-x))


def _silu(x):
    return x * _sigmoid(x)


def _softplus(x):
    return jnp.maximum(x, 0.0) + jnp.log(1.0 + jnp.exp(-jnp.abs(x)))


def _dot(a, b):
    return jnp.dot(a, b, preferred_element_type=F32)


def _dot_nt(a, b):
    return lax.dot_general(a, b, (((1,), (1,)), ((), ())), preferred_element_type=F32)


def _params(sem):
    return pltpu.CompilerParams(dimension_semantics=sem, vmem_limit_bytes=VMEM_LIMIT)


def _adaln_kernel(c_ref, w_ref, b_ref, o_ref):
    s = _silu(c_ref[...]).astype(BF16)
    o_ref[...] = _dot(s, w_ref[...].astype(BF16)) + b_ref[...]


def _adaln(c, ada_w, ada_b, tn=1024):
    depth, d, n = ada_w.shape
    rows = c.shape[0]
    return pl.pallas_call(
        _adaln_kernel,
        grid=(depth, n // tn),
        in_specs=[
            pl.BlockSpec((rows, d), lambda l, j: (0, 0)),
            pl.BlockSpec((None, d, tn), lambda l, j: (l, 0, j)),
            pl.BlockSpec((None, 1, tn), lambda l, j: (l, 0, j)),
        ],
        out_specs=pl.BlockSpec((None, rows, tn), lambda l, j: (l, 0, j)),
        out_shape=jax.ShapeDtypeStruct((depth, rows, n), F32),
        compiler_params=_params(("arbitrary", "arbitrary")),
    )(c, ada_w, ada_b.reshape(depth, 1, n))


def _modulated_norm(x, nw, sc, sh):
    y = x * lax.rsqrt(jnp.mean(x * x, axis=-1, keepdims=True) + EPS) * nw
    return y * (1.0 + sc) + sh


def _in_proj_kernel(x_ref, nw_ref, sc_ref, sh_ref, w_ref, wba_ref, alog_ref, dt_ref, o_ref, gate_ref, h_ref):
    @pl.when(pl.program_id(2) == 0)
    def _():
        h = _modulated_norm(x_ref[...], nw_ref[...], sc_ref[...], sh_ref[...]).astype(BF16)
        h_ref[...] = h
        ba = _dot(h, wba_ref[...])
        lane = lax.broadcasted_iota(jnp.int32, ba.shape, 1)
        gate_ref[...] = jnp.where(lane < 2 * GDN_HEADS, _sigmoid(ba),
                                  -jnp.exp(alog_ref[...]) * _softplus(ba + dt_ref[...]))

    o_ref[...] = _dot(h_ref[...], w_ref[...]).astype(o_ref.dtype)


def _in_proj(x, nw, sc, sh, w_main, w_ba, alog_row, dt_row, tm=512, tn=1408):
    b, t, d = x.shape
    n = w_main.shape[1]
    nba = w_ba.shape[1]
    return pl.pallas_call(
        _in_proj_kernel,
        grid=(b, t // tm, n // tn),
        in_specs=[
            pl.BlockSpec((None, tm, d), lambda bi, i, j: (bi, i, 0)),
            pl.BlockSpec((1, d), lambda bi, i, j: (0, 0)),
            pl.BlockSpec((None, 1, d), lambda bi, i, j: (bi, 0, 0)),
            pl.BlockSpec((None, 1, d), lambda bi, i, j: (bi, 0, 0)),
            pl.BlockSpec((d, tn), lambda bi, i, j: (0, j)),
            pl.BlockSpec((d, nba), lambda bi, i, j: (0, 0)),
            pl.BlockSpec((1, nba), lambda bi, i, j: (0, 0)),
            pl.BlockSpec((1, nba), lambda bi, i, j: (0, 0)),
        ],
        out_specs=[
            pl.BlockSpec((None, tm, tn), lambda bi, i, j: (bi, i, j)),
            pl.BlockSpec((None, tm, nba), lambda bi, i, j: (bi, i, 0)),
        ],
        out_shape=[
            jax.ShapeDtypeStruct((b, t, n), BF16),
            jax.ShapeDtypeStruct((b, t, nba), F32),
        ],
        scratch_shapes=[pltpu.VMEM((tm, d), BF16)],
        compiler_params=_params(("arbitrary", "arbitrary", "arbitrary")),
    )(x, nw, sc, sh, w_main, w_ba, alog_row, dt_row)


CONV_TILE = 256
LANE_BETA = (0, 8)
LANE_G = (16, 24)


def _gdn_kernel(q_ref, k_ref, v_ref, z_ref, gate_ref, cwq_ref, cwk_ref, cwv_ref, nw_ref,
                o_ref,
                pad_ref, qs_ref, ks_ref, vs_ref, gsel_ref, gc_ref, gx_ref, grow_ref,
                lhs_ref, nb_ref, gl_ref, oacc_ref):
    t = q_ref.shape[0]
    n_tiles = t // CONV_TILE
    n_pairs = t // PAIR
    n_chunks = t // CHUNK
    n_win = t // WIN
    head = pl.program_id(1)

    halo = 8
    pad_ref[0:halo, :] = jnp.zeros((halo, HEAD_DIM), F32)
    pad_ref[halo + t:halo + t + halo, :] = jnp.zeros((halo, HEAD_DIM), F32)

    def conv_silu(src_ref, cw_ref, dst_ref, normalize, scale):
        def fill(i, c):
            r = pl.multiple_of(i * CONV_TILE, CONV_TILE)
            pad_ref[pl.ds(halo + r, CONV_TILE), :] = src_ref[pl.ds(r, CONV_TILE), :].astype(F32)
            return c

        lax.fori_loop(0, n_tiles, fill, 0)
        cw = cw_ref[...]

        def body(i, c):
            r = pl.multiple_of(i * CONV_TILE, CONV_TILE)
            win = pad_ref[pl.ds(r, CONV_TILE + 2 * halo), :]
            acc = None
            for j in range(CONV_K):
                off = halo - CONV_K // 2 + j
                term = win[off:off + CONV_TILE, :] * cw[j:j + 1, :]
                acc = term if acc is None else acc + term
            y = _silu(acc)
            if normalize:
                y = y * lax.rsqrt(jnp.sum(y * y, axis=-1, keepdims=True) + EPS)
            if scale != 1.0:
                y = y * scale
            dst_ref[pl.ds(r, CONV_TILE), :] = y
            return c

        lax.fori_loop(0, n_tiles, body, 0)

    conv_silu(q_ref, cwq_ref, qs_ref, True, HEAD_DIM ** -0.5)
    conv_silu(k_ref, cwk_ref, ks_ref, True, 1.0)
    conv_silu(v_ref, cwv_ref, vs_ref, False, 1.0)

    ri = lax.broadcasted_iota(jnp.int32, (PAIR, PAIR), 0)
    ci = lax.broadcasted_iota(jnp.int32, (PAIR, PAIR), 1)
    same = (ri // CHUNK) == (ci // CHUNK)
    m_low = jnp.where(same & (ci <= ri), 1.0, 0.0).astype(BF16)
    m_up = jnp.where(same & (ci >= ri), 1.0, 0.0).astype(BF16)
    m_cum = jnp.concatenate([m_low, m_up], axis=0)

    def cum_body(i, c):
        r = pl.multiple_of(i * PAIR, PAIR)
        x = gate_ref[pl.ds(r, PAIR), :]
        for bit in (1, 2, 4):
            x = jnp.where((head & bit) != 0, pltpu.roll(x, HEAD_DIM - bit, 1), x)
        gsel_ref[pl.ds(r, PAIR), :] = x
        hi = x.astype(BF16)
        r1 = x - hi.astype(F32)
        mid = r1.astype(BF16)
        lo = (r1 - mid.astype(F32)).astype(BF16)
        cum = _dot(m_cum, jnp.concatenate([hi, mid, lo], axis=1))
        cum = cum[:, :HEAD_DIM] + cum[:, HEAD_DIM:2 * HEAD_DIM] + cum[:, 2 * HEAD_DIM:]
        pre, suf = cum[:PAIR], cum[PAIR:]
        bwd_lane = lax.broadcasted_iota(jnp.int32, x.shape, 1) == LANE_G[1]
        gc = jnp.where(bwd_lane, suf, pre)
        gc_ref[pl.ds(r, PAIR), :] = gc
        gx_ref[pl.ds(r, PAIR), :] = jnp.where(bwd_lane, pre, suf) - x
        gct = gc.T
        grow_ref[i, 0:1, :] = gct[LANE_G[0]:LANE_G[0] + 1, :]
        grow_ref[i, 1:2, :] = gct[LANE_G[1]:LANE_G[1] + 1, :]
        return c

    lax.fori_loop(0, n_pairs, cum_body, 0, unroll=2)

    def oacc_zero(i, c):
        r = pl.multiple_of(i * CONV_TILE, CONV_TILE)
        oacc_ref[pl.ds(r, CONV_TILE), :] = jnp.zeros((CONV_TILE, HEAD_DIM), F32)
        return c

    lax.fori_loop(0, n_tiles, oacc_zero, 0)

    def window_body(step, c):
        rw = lax.broadcasted_iota(jnp.int32, (WIN, WIN), 0)
        cw = lax.broadcasted_iota(jnp.int32, (WIN, WIN), 1)
        eye = jnp.where(rw == cw, 1.0, 0.0)

        def same_block(n):
            return (rw // n) == (cw // n)

        chains = []
        for wi in range(WINDOWS_PER_STEP):
            w = step * WINDOWS_PER_STEP + wi
            rows = pl.ds(pl.multiple_of(w * WIN, WIN), WIN)
            q = qs_ref[rows, :]
            k = ks_ref[rows, :]
            v = vs_ref[rows, :]
            gs = gsel_ref[rows, :]
            gcv = gc_ref[rows, :]
            gxv = gx_ref[rows, :]
            growt = jnp.concatenate([grow_ref[2 * w], grow_ref[2 * w + 1]], axis=1)
            kbf = k.astype(BF16)
            for d in range(2):
                beta = gs[:, LANE_BETA[d]:LANE_BETA[d] + 1]
                gcol = gcv[:, LANE_G[d]:LANE_G[d] + 1]
                eg = jnp.exp(gcol)
                kb = k * beta
                incl = same_block(CHUNK) & ((rw >= cw) if d == 0 else (rw <= cw))
                decay = jnp.where(incl, jnp.exp(jnp.where(incl, gcol - growt[d:d + 1, :], 0.0)), 0.0)
                gram = _dot_nt(jnp.concatenate([kb, q], axis=0).astype(BF16), kbf)
                chains.append(dict(
                    w=w, d=d, rows=rows, gcol=gcol,
                    rhs=jnp.concatenate([v * beta, kb * eg], axis=1).astype(BF16),
                    qd=q * eg,
                    kd=k * jnp.exp(gxv[:, LANE_G[d]:LANE_G[d] + 1]),
                    lmat=jnp.where(rw != cw, gram[:WIN] * decay, 0.0),
                    amat=(gram[WIN:] * decay).astype(BF16)))

        base = same_block(BASE_BLOCK)
        l4 = [jnp.where(base, ch["lmat"], 0.0) for ch in chains]
        l4b = [x.astype(BF16) for x in l4]
        sq = [_dot(x, x) for x in l4b]
        dinv = [(eye - x) + _dot((eye - x).astype(BF16), s.astype(BF16)) for x, s in zip(l4, sq)]
        n = BASE_BLOCK
        while n < CHUNK:
            off = same_block(2 * n) & jnp.logical_not(same_block(n))
            lo = [jnp.where(off, ch["lmat"], 0.0).astype(BF16) for ch in chains]
            db = [x.astype(BF16) for x in dinv]
            xs = [_dot(a, b) for a, b in zip(db, lo)]
            dinv = [a - _dot(x.astype(BF16), b) for a, x, b in zip(dinv, xs, db)]
            n *= 2

        colchunk = lax.broadcasted_iota(jnp.int32, (HEAD_DIM, WIN), 1) // CHUNK
        uws = [_dot(a.astype(BF16), ch["rhs"]).astype(BF16) for a, ch in zip(dinv, chains)]
        auws = [_dot(ch["amat"], uw) for ch, uw in zip(chains, uws)]
        mns = []
        for ch, uw in zip(chains, uws):
            kd = ch["kd"]
            kdt = jnp.concatenate([kd[:PAIR].T, kd[PAIR:].T], axis=1)
            lhs4 = jnp.concatenate([jnp.where(colchunk == j, kdt, 0.0) for j in range(WIN // CHUNK)],
                                   axis=0).astype(BF16)
            mns.append(_dot(lhs4, uw))
        for ch, auw, mn in zip(chains, auws, mns):
            d = ch["d"]
            oacc_ref[ch["rows"], :] += auw[:, :HEAD_DIM]
            qp = (ch["qd"] - auw[:, HEAD_DIM:]).astype(BF16)
            for j in range(WIN // CHUNK):
                cidx = (WIN // CHUNK) * ch["w"] + j
                it = cidx if d == 0 else n_chunks - 1 - cidx
                blk = mn[j * HEAD_DIM:(j + 1) * HEAD_DIM]
                lo_row = d * LHS_ROWS
                lhs_ref[it, lo_row:lo_row + HEAD_DIM, :] = (-blk[:, HEAD_DIM:]).astype(BF16)
                lhs_ref[it, lo_row + HEAD_DIM:lo_row + LHS_ROWS, :] = qp[j * CHUNK:(j + 1) * CHUNK]
                nb_ref[it, :, d * HEAD_DIM:(d + 1) * HEAD_DIM] = blk[:, :HEAD_DIM]
                last = j * CHUNK + (CHUNK - 1 if d == 0 else 0)
                gl_ref[it, :, d * HEAD_DIM:(d + 1) * HEAD_DIM] = jnp.broadcast_to(
                    jnp.exp(ch["gcol"][last:last + 1, :]), (8, HEAD_DIM))
        return c

    lax.fori_loop(0, n_win // WINDOWS_PER_STEP, window_body, 0)

    def sweep_body(it, s):
        rf = pl.multiple_of(it * CHUNK, CHUNK)
        rb = pl.multiple_of((n_chunks - 1 - it) * CHUNK, CHUNK)
        r = _dot(lhs_ref[it], s.astype(BF16))
        upd = jnp.concatenate([r[0:HEAD_DIM, 0:HEAD_DIM],
                               r[LHS_ROWS:LHS_ROWS + HEAD_DIM, HEAD_DIM:]], axis=1)
        oacc_ref[pl.ds(rf, CHUNK), :] += r[HEAD_DIM:LHS_ROWS, 0:HEAD_DIM]
        oacc_ref[pl.ds(rb, CHUNK), :] += r[LHS_ROWS + HEAD_DIM:, HEAD_DIM:]
        return s * gl_ref[it][0:1, :] + upd + nb_ref[it]

    lax.fori_loop(0, n_chunks, sweep_body, jnp.zeros((HEAD_DIM, 2 * HEAD_DIM), F32))

    nw = nw_ref[...]

    def out_body(i, c):
        r = pl.multiple_of(i * CONV_TILE, CONV_TILE)
        o = oacc_ref[pl.ds(r, CONV_TILE), :]
        y = o * lax.rsqrt(jnp.mean(o * o, axis=-1, keepdims=True) + EPS) * nw
        z = z_ref[pl.ds(r, CONV_TILE), :].astype(F32)
        o_ref[pl.ds(r, CONV_TILE), :] = (y * _silu(z)).astype(o_ref.dtype)
        return c

    lax.fori_loop(0, n_tiles, out_body, 0)


def _gdn(p_main, gates, conv_w, norm_w):
    b, t, _ = p_main.shape
    h = GDN_HEADS
    n_chunks = t // CHUNK
    n_pairs = t // PAIR
    col = lambda off: (lambda bi, hi: (bi, 0, off + hi))
    cwcol = lambda off: (lambda bi, hi: (0, off + hi))
    const2 = lambda bi, hi: (0, 0)
    return pl.pallas_call(
        _gdn_kernel,
        grid=(b, h),
        in_specs=[
            pl.BlockSpec((None, t, HEAD_DIM), col(0)),
            pl.BlockSpec((None, t, HEAD_DIM), col(h)),
            pl.BlockSpec((None, t, HEAD_DIM), col(2 * h)),
            pl.BlockSpec((None, t, HEAD_DIM), col(3 * h)),
            pl.BlockSpec((None, t, HEAD_DIM), lambda bi, hi: (bi, 0, 0)),
            pl.BlockSpec((CONV_K, HEAD_DIM), cwcol(0)),
            pl.BlockSpec((CONV_K, HEAD_DIM), cwcol(h)),
            pl.BlockSpec((CONV_K, HEAD_DIM), cwcol(2 * h)),
            pl.BlockSpec((1, HEAD_DIM), const2),
        ],
        out_specs=pl.BlockSpec((None, t, HEAD_DIM), lambda bi, hi: (bi, 0, hi)),
        out_shape=jax.ShapeDtypeStruct((b, t, h * HEAD_DIM), BF16),
        scratch_shapes=[
            pltpu.VMEM((t + 16, HEAD_DIM), F32),
            pltpu.VMEM((t, HEAD_DIM), F32),
            pltpu.VMEM((t, HEAD_DIM), F32),
            pltpu.VMEM((t, HEAD_DIM), F32),
            pltpu.VMEM((t, HEAD_DIM), F32),
            pltpu.VMEM((t, HEAD_DIM), F32),
            pltpu.VMEM((t, HEAD_DIM), F32),
            pltpu.VMEM((n_pairs, 8, HEAD_DIM), F32),
            pltpu.VMEM((n_chunks, 2 * LHS_ROWS, HEAD_DIM), BF16),
            pltpu.VMEM((n_chunks, HEAD_DIM, 2 * HEAD_DIM), F32),
            pltpu.VMEM((n_chunks, 8, 2 * HEAD_DIM), F32),
            pltpu.VMEM((t, HEAD_DIM), F32),
        ],
        compiler_params=_params(("arbitrary", "arbitrary")),
    )(p_main, p_main, p_main, p_main, gates, conv_w, conv_w, conv_w, norm_w)


def _attn_prep_kernel(q_ref, kv_ref, qw_ref, kw_ref, cos_ref, sa_ref, sb_ref, qo_ref, ko_ref):
    cos = cos_ref[...]
    sa = sa_ref[...]
    sb = sb_ref[...]

    def norm_rope(x, w):
        y = x * lax.rsqrt(jnp.mean(x * x, axis=-1, keepdims=True) + EPS) * w
        return y * cos + pltpu.roll(y, HEAD_DIM - ROT_HALF // 2, 1) * sa + pltpu.roll(y, ROT_HALF // 2, 1) * sb

    qw = qw_ref[...]
    kw = kw_ref[...]
    scale = HEAD_DIM ** -0.5
    for hh in range(ATT_HEADS):
        sl = slice(hh * HEAD_DIM, (hh + 1) * HEAD_DIM)
        qo_ref[:, sl] = (norm_rope(q_ref[:, sl].astype(F32), qw) * scale).astype(qo_ref.dtype)
    for hh in range(ATT_KV_HEADS):
        sl = slice(hh * HEAD_DIM, (hh + 1) * HEAD_DIM)
        ko_ref[:, sl] = norm_rope(kv_ref[:, sl].astype(F32), kw).astype(ko_ref.dtype)


def _attn_prep(p_main, q_off, qw, kw, cos, sa, sb, tq=256):
    b, t, _ = p_main.shape
    qwid = ATT_HEADS * HEAD_DIM
    kvwid = 2 * ATT_KV_HEADS * HEAD_DIM
    row = lambda bi, i: (i, 0)
    const2 = lambda bi, i: (0, 0)
    return pl.pallas_call(
        _attn_prep_kernel,
        grid=(b, t // tq),
        in_specs=[
            pl.BlockSpec((None, tq, qwid), lambda bi, i: (bi, i, q_off // qwid)),
            pl.BlockSpec((None, tq, kvwid), lambda bi, i: (bi, i, (q_off + qwid) // kvwid)),
            pl.BlockSpec((1, HEAD_DIM), const2),
            pl.BlockSpec((1, HEAD_DIM), const2),
            pl.BlockSpec((tq, HEAD_DIM), row),
            pl.BlockSpec((tq, HEAD_DIM), row),
            pl.BlockSpec((tq, HEAD_DIM), row),
        ],
        out_specs=[
            pl.BlockSpec((None, tq, qwid), lambda bi, i: (bi, i, 0)),
            pl.BlockSpec((None, tq, ATT_KV_HEADS * HEAD_DIM), lambda bi, i: (bi, i, 0)),
        ],
        out_shape=[
            jax.ShapeDtypeStruct((b, t, qwid), BF16),
            jax.ShapeDtypeStruct((b, t, ATT_KV_HEADS * HEAD_DIM), BF16),
        ],
        compiler_params=_params(("arbitrary", "arbitrary")),
    )(p_main, p_main, qw, kw, cos, sa, sb)


def _attn_kernel(q_ref, k_ref, v_ref, o_ref, m_ref, l_ref, acc_ref):
    j = pl.program_id(3)
    tq = q_ref.shape[0]

    @pl.when(j == 0)
    def _():
        m_ref[...] = jnp.full(m_ref.shape, -jnp.inf, F32)
        l_ref[...] = jnp.zeros(l_ref.shape, F32)
        acc_ref[...] = jnp.zeros(acc_ref.shape, F32)

    q = jnp.concatenate([q_ref[:, g * HEAD_DIM:(g + 1) * HEAD_DIM] for g in range(ATT_GROUPS)], axis=0)
    s = _dot_nt(q, k_ref[...])
    m_prev = m_ref[...]
    m_new = jnp.maximum(m_prev, jnp.max(s, axis=-1, keepdims=True))
    alpha = jnp.exp(m_prev - m_new)
    p = jnp.exp(s - m_new[:, 0:1])
    l_ref[...] = alpha * l_ref[...] + jnp.sum(p, axis=-1, keepdims=True)
    acc_ref[...] = alpha * acc_ref[...] + _dot(p.astype(BF16), v_ref[...])
    m_ref[...] = m_new

    @pl.when(j == pl.num_programs(3) - 1)
    def _():
        o = acc_ref[...] / l_ref[...]
        for g in range(ATT_GROUPS):
            o_ref[:, g * HEAD_DIM:(g + 1) * HEAD_DIM] = o[g * tq:(g + 1) * tq, :].astype(o_ref.dtype)


def _attention(q_rot, k_rot, p_main, v_off, tq=128, tk=1024):
    b, t, _ = q_rot.shape
    tk = min(tk, t)
    gw = ATT_GROUPS * HEAD_DIM
    vblk = v_off // HEAD_DIM
    return pl.pallas_call(
        _attn_kernel,
        grid=(b, ATT_KV_HEADS, t // tq, t // tk),
        in_specs=[
            pl.BlockSpec((None, tq, gw), lambda bi, hi, i, j: (bi, i, hi)),
            pl.BlockSpec((None, tk, HEAD_DIM), lambda bi, hi, i, j: (bi, j, hi)),
            pl.BlockSpec((None, tk, HEAD_DIM), lambda bi, hi, i, j: (bi, j, vblk + hi)),
        ],
        out_specs=pl.BlockSpec((None, tq, gw), lambda bi, hi, i, j: (bi, i, hi)),
        out_shape=jax.ShapeDtypeStruct((b, t, ATT_HEADS * HEAD_DIM), BF16),
        scratch_shapes=[
            pltpu.VMEM((ATT_GROUPS * tq, HEAD_DIM), F32),
            pltpu.VMEM((ATT_GROUPS * tq, HEAD_DIM), F32),
            pltpu.VMEM((ATT_GROUPS * tq, HEAD_DIM), F32),
        ],
        compiler_params=_params(("arbitrary", "arbitrary", "arbitrary", "arbitrary")),
    )(q_rot, k_rot, p_main)


def _out_proj_kernel(ya_ref, yb_ref, wa_ref, wb_ref, x_ref, gt_ref, o_ref):
    mix = _dot(ya_ref[...], wa_ref[...]) + _dot(yb_ref[...], wb_ref[...])
    o_ref[...] = x_ref[...] + gt_ref[...] * mix


def _out_proj(ya, yb, w_out, x, gt, tm=512):
    b, t, d = x.shape
    ka = ya.shape[2]
    kb = yb.shape[2]
    return pl.pallas_call(
        _out_proj_kernel,
        grid=(b, t // tm),
        in_specs=[
            pl.BlockSpec((None, tm, ka), lambda bi, i: (bi, i, 0)),
            pl.BlockSpec((None, tm, kb), lambda bi, i: (bi, i, 0)),
            pl.BlockSpec((ka, d), lambda bi, i: (0, 0)),
            pl.BlockSpec((kb, d), lambda bi, i: (ka // kb, 0)),
            pl.BlockSpec((None, tm, d), lambda bi, i: (bi, i, 0)),
            pl.BlockSpec((None, 1, d), lambda bi, i: (bi, 0, 0)),
        ],
        out_specs=pl.BlockSpec((None, tm, d), lambda bi, i: (bi, i, 0)),
        out_shape=jax.ShapeDtypeStruct((b, t, d), F32),
        compiler_params=_params(("arbitrary", "arbitrary")),
    )(ya, yb, w_out, w_out, x, gt)


def _mlp_kernel(x_ref, nw_ref, sc_ref, sh_ref, gt_ref, wu_ref, wd_ref, o_ref, h_ref, acc_ref):
    f = pl.program_id(2)

    @pl.when(f == 0)
    def _():
        h_ref[...] = _modulated_norm(x_ref[...], nw_ref[...], sc_ref[...], sh_ref[...]).astype(BF16)
        acc_ref[...] = jnp.zeros(acc_ref.shape, F32)

    u = jnp.maximum(_dot(h_ref[...], wu_ref[...]), 0.0)
    acc_ref[...] += _dot((u * u).astype(BF16), wd_ref[...])

    @pl.when(f == pl.num_programs(2) - 1)
    def _():
        o_ref[...] = x_ref[...] + gt_ref[...] * acc_ref[...]


def _mlp(x, nw, sc, sh, gt, w_up, w_down, tm=512, tf=512):
    b, t, d = x.shape
    dff = w_up.shape[1]
    vec = lambda bi, i, f: (bi, 0, 0)
    return pl.pallas_call(
        _mlp_kernel,
        grid=(b, t // tm, dff // tf),
        in_specs=[
            pl.BlockSpec((None, tm, d), lambda bi, i, f: (bi, i, 0)),
            pl.BlockSpec((1, d), lambda bi, i, f: (0, 0)),
            pl.BlockSpec((None, 1, d), vec),
            pl.BlockSpec((None, 1, d), vec),
            pl.BlockSpec((None, 1, d), vec),
            pl.BlockSpec((d, tf), lambda bi, i, f: (0, f)),
            pl.BlockSpec((tf, d), lambda bi, i, f: (f, 0)),
        ],
        out_specs=pl.BlockSpec((None, tm, d), lambda bi, i, f: (bi, i, 0)),
        out_shape=jax.ShapeDtypeStruct((b, t, d), F32),
        scratch_shapes=[pltpu.VMEM((tm, d), BF16), pltpu.VMEM((tm, d), F32)],
        compiler_params=_params(("arbitrary", "arbitrary", "arbitrary")),
    )(x, nw, sc, sh, gt, w_up, w_down)


def _rope_tables(t):
    rows = t // GRID_W
    row = jnp.repeat(jnp.arange(rows, dtype=F32), GRID_W)
    col = jnp.tile(jnp.arange(GRID_W, dtype=F32), rows)
    inv_freq = ROPE_THETA ** (-jnp.arange(0, ROT_HALF, 2, dtype=F32) / ROT_HALF)
    ang_r = row[:, None] * inv_freq[None, :]
    ang_c = col[:, None] * inv_freq[None, :]
    ang = jnp.concatenate([ang_r, ang_r, ang_c, ang_c], axis=-1)
    cos, sin = jnp.cos(ang), jnp.sin(ang)
    first = (jnp.arange(HEAD_DIM) % ROT_HALF) < ROT_HALF // 2
    return cos, jnp.where(first, -sin, 0.0), jnp.where(first, 0.0, sin)


def _gate_rows(a_log, dt_bias):
    def place(p):
        return jnp.pad(p.reshape(1, 2 * GDN_HEADS), ((0, 0), (2 * GDN_HEADS, HEAD_DIM - 4 * GDN_HEADS)))
    return place(a_log.astype(F32)), place(dt_bias.astype(F32))


def _trunk_layer(x, mod, norm1_w, norm2_w, w_main, w_ba, conv_w, alog_row, dt_row, gdn_norm_w,
                 q_norm_w, k_norm_w, w_out, w_up, w_down, rope):
    b, t, d = x.shape
    sh1, sc1, gt1, sh2, sc2, gt2 = [m.reshape(b, 1, d) for m in jnp.split(mod, N_MOD, axis=-1)]
    gdn_w = GDN_HEADS * HEAD_DIM
    p_main, gates = _in_proj(x, norm1_w, sc1, sh1, w_main, w_ba, alog_row, dt_row)
    ya = _gdn(p_main, gates, conv_w, gdn_norm_w)
    att_off = 4 * gdn_w
    q_rot, k_rot = _attn_prep(p_main, att_off, q_norm_w, k_norm_w, *rope)
    v_off = att_off + (ATT_HEADS + ATT_KV_HEADS) * HEAD_DIM
    yb = _attention(q_rot, k_rot, p_main, v_off)
    x = _out_proj(ya, yb, w_out, x, gt1)
    return _mlp(x, norm2_w, sc2, sh2, gt2, w_up, w_down)


def kernel(x_prompt, x_sample, c_prompt, c_sample, ada_w, ada_b, norm1_w, norm2_w, w_in, conv_w, a_log,
           dt_bias, gdn_norm_w, q_norm_w, k_norm_w, w_out, w_up, w_down):
    depth = ada_w.shape[0]
    gdn_w = GDN_HEADS * HEAD_DIM
    n_gate = 4 * GDN_HEADS
    nb_prompt = c_prompt.shape[0]

    mod = _adaln(jnp.concatenate([c_prompt, c_sample], axis=0), ada_w, ada_b)

    w_main = jnp.concatenate([w_in[:, :, :4 * gdn_w], w_in[:, :, 4 * gdn_w + n_gate:]], axis=-1).astype(BF16)
    w_ba = jnp.pad(w_in[:, :, 4 * gdn_w:4 * gdn_w + n_gate], ((0, 0), (0, 0), (0, HEAD_DIM - n_gate))).astype(BF16)
    w_out_b = w_out.astype(BF16)
    w_up_b = w_up.astype(BF16)
    w_down_b = w_down.astype(BF16)

    xs = [x_prompt, x_sample]
    ropes = [_rope_tables(x.shape[1]) for x in xs]
    for l in range(depth):
        alog_row, dt_row = _gate_rows(a_log[l], dt_bias[l])
        mods = [mod[l, :nb_prompt], mod[l, nb_prompt:]]
        for i in range(2):
            xs[i] = _trunk_layer(
                xs[i], mods[i], norm1_w[l].reshape(1, -1), norm2_w[l].reshape(1, -1), w_main[l], w_ba[l],
                conv_w[l], alog_row, dt_row, gdn_norm_w[l].reshape(1, -1), q_norm_w[l].reshape(1, -1),
                k_norm_w[l].reshape(1, -1), w_out_b[l], w_up_b[l], w_down_b[l], ropes[i])
    return (xs[0], xs[1])
```

```python
import functools

import jax
import jax.numpy as jnp
from jax import lax
from jax.experimental import pallas as pl
from jax.experimental.pallas import tpu as pltpu

F32 = jnp.float32
BF16 = jnp.bfloat16

HEAD_DIM = 128
GDN_HEADS = 8
ATT_HEADS = 8
ATT_KV_HEADS = 2
ATT_GROUPS = ATT_HEADS // ATT_KV_HEADS
CONV_K = 5
CHUNK = 64
PAIR = 2 * CHUNK
WIN = 4 * CHUNK
LHS_ROWS = HEAD_DIM + CHUNK
GRID_W = 64
ROPE_THETA = 10000.0
ROT_HALF = HEAD_DIM // 2
N_MOD = 6
EPS = 1e-6
LOG2_E = 1.4426950408889634
BASE_BLOCK = 4
WINDOWS_PER_STEP = 2
STEPS_PER_WINDOW_STEP = WINDOWS_PER_STEP * (WIN // CHUNK)

VMEM_LIMIT = 56 * 1024 * 1024


def _sigmoid(x):
    return 1.0 / (1.0 + jnp.exp(-x))


def _silu(x):
    return x * _sigmoid(x)


def _softplus(x):
    return jnp.maximum(x, 0.0) + jnp.log(1.0 + jnp.exp(-jnp.abs(x)))


def _dot(a, b):
    return jnp.dot(a, b, preferred_element_type=F32)


def _dot_nt(a, b):
    return lax.dot_general(a, b, (((1,), (1,)), ((), ())), preferred_element_type=F32)


def _params(sem):
    return pltpu.CompilerParams(dimension_semantics=sem, vmem_limit_bytes=VMEM_LIMIT)


def _adaln_kernel(c_ref, w_ref, b_ref, o_ref):
    s = _silu(c_ref[...]).astype(BF16)
    o_ref[...] = _dot(s, w_ref[...].astype(BF16)) + b_ref[...]


def _adaln(c, ada_w, ada_b, tn=1024):
    depth, d, n = ada_w.shape
    rows = c.shape[0]
    return pl.pallas_call(
        _adaln_kernel,
        grid=(depth, n // tn),
        in_specs=[
            pl.BlockSpec((rows, d), lambda l, j: (0, 0)),
            pl.BlockSpec((None, d, tn), lambda l, j: (l, 0, j)),
            pl.BlockSpec((None, 1, tn), lambda l, j: (l, 0, j)),
        ],
        out_specs=pl.BlockSpec((None, rows, tn), lambda l, j: (l, 0, j)),
        out_shape=jax.ShapeDtypeStruct((depth, rows, n), F32),
        compiler_params=_params(("arbitrary", "arbitrary")),
    )(c, ada_w, ada_b.reshape(depth, 1, n))


def _modulated_norm(x, nw, sc, sh):
    y = x * lax.rsqrt(jnp.mean(x * x, axis=-1, keepdims=True) + EPS) * nw
    return y * (1.0 + sc) + sh


def _in_proj_kernel(x_ref, nw_ref, sc_ref, sh_ref, w_ref, wba_ref, alog_ref, dt_ref, o_ref, gate_ref, h_ref):
    @pl.when(pl.program_id(2) == 0)
    def _():
        h = _modulated_norm(x_ref[...], nw_ref[...], sc_ref[...], sh_ref[...]).astype(BF16)
        h_ref[...] = h
        ba = _dot(h, wba_ref[...])
        lane = lax.broadcasted_iota(jnp.int32, ba.shape, 1)
        gate_ref[...] = jnp.where(lane < 2 * GDN_HEADS, _sigmoid(ba),
                                  -jnp.exp(alog_ref[...]) * _softplus(ba + dt_ref[...]))

    o_ref[...] = _dot(h_ref[...], w_ref[...]).astype(o_ref.dtype)


def _in_proj(x, nw, sc, sh, w_main, w_ba, alog_row, dt_row, tm=512, tn=1408):
    b, t, d = x.shape
    n = w_main.shape[1]
    nba = w_ba.shape[1]
    return pl.pallas_call(
        _in_proj_kernel,
        grid=(b, t // tm, n // tn),
        in_specs=[
            pl.BlockSpec((None, tm, d), lambda bi, i, j: (bi, i, 0)),
            pl.BlockSpec((1, d), lambda bi, i, j: (0, 0)),
            pl.BlockSpec((None, 1, d), lambda bi, i, j: (bi, 0, 0)),
            pl.BlockSpec((None, 1, d), lambda bi, i, j: (bi, 0, 0)),
            pl.BlockSpec((d, tn), lambda bi, i, j: (0, j)),
            pl.BlockSpec((d, nba), lambda bi, i, j: (0, 0)),
            pl.BlockSpec((1, nba), lambda bi, i, j: (0, 0)),
            pl.BlockSpec((1, nba), lambda bi, i, j: (0, 0)),
        ],
        out_specs=[
            pl.BlockSpec((None, tm, tn), lambda bi, i, j: (bi, i, j)),
            pl.BlockSpec((None, tm, nba), lambda bi, i, j: (bi, i, 0)),
        ],
        out_shape=[
            jax.ShapeDtypeStruct((b, t, n), BF16),
            jax.ShapeDtypeStruct((b, t, nba), F32),
        ],
        scratch_shapes=[pltpu.VMEM((tm, d), BF16)],
        compiler_params=_params(("arbitrary", "arbitrary", "arbitrary")),
    )(x, nw, sc, sh, w_main, w_ba, alog_row, dt_row)


CONV_TILE = 256
CUM_TILES = 4
LANE_BETA = (0, 8)
LANE_G = (16, 24)


def _gdn_kernel(q_ref, k_ref, v_ref, z_ref, gate_ref, cwq_ref, cwk_ref, cwv_ref, nw_ref,
                o_ref,
                pad_ref, qs_ref, ks_ref, vs_ref, gsel_ref, gc_ref, gx_ref, grow_ref,
                lhs_ref, nb_ref, gl_ref, oacc_ref):
    t = q_ref.shape[0]
    n_tiles = t // CONV_TILE
    n_pairs = t // PAIR
    n_chunks = t // CHUNK
    n_win = t // WIN
    head = pl.program_id(1)

    halo = 8
    pad_ref[0:halo, :] = jnp.zeros((halo, HEAD_DIM), F32)
    pad_ref[halo + t:halo + t + halo, :] = jnp.zeros((halo, HEAD_DIM), F32)

    def conv_silu(src_ref, cw_ref, dst_ref, normalize, scale):
        def fill(i, c):
            r = pl.multiple_of(i * CONV_TILE, CONV_TILE)
            pad_ref[pl.ds(halo + r, CONV_TILE), :] = src_ref[pl.ds(r, CONV_TILE), :].astype(F32)
            return c

        lax.fori_loop(0, n_tiles, fill, 0)
        cw = cw_ref[...]

        def body(i, c):
            starts = [pl.multiple_of((2 * i + j) * CONV_TILE, CONV_TILE) for j in range(2)]
            accs = [None, None]
            for j in range(CONV_K):
                off = halo - CONV_K // 2 + j
                for n, r in enumerate(starts):
                    term = pad_ref[pl.ds(r + off, CONV_TILE), :] * cw[j:j + 1, :]
                    accs[n] = term if accs[n] is None else accs[n] + term
            ys = [_silu(acc) for acc in accs]
            if normalize:
                norms = [lax.rsqrt(jnp.sum(y * y, axis=-1, keepdims=True) + EPS) for y in ys]
                ys = [y * nrm for y, nrm in zip(ys, norms)]
            if scale != 1.0:
                ys = [y * scale for y in ys]
            for r, y in zip(starts, ys):
                dst_ref[pl.ds(r, CONV_TILE), :] = y
            return c

        lax.fori_loop(0, n_tiles // 2, body, 0)

    conv_silu(q_ref, cwq_ref, qs_ref, True, HEAD_DIM ** -0.5)
    conv_silu(k_ref, cwk_ref, ks_ref, True, 1.0)
    conv_silu(v_ref, cwv_ref, vs_ref, False, 1.0)

    ri = lax.broadcasted_iota(jnp.int32, (PAIR, PAIR), 0)
    ci = lax.broadcasted_iota(jnp.int32, (PAIR, PAIR), 1)
    same = (ri // CHUNK) == (ci // CHUNK)
    m_low = jnp.where(same & (ci <= ri), 1.0, 0.0).astype(BF16)
    m_up = jnp.where(same & (ci >= ri), 1.0, 0.0).astype(BF16)
    m_cum = jnp.concatenate([m_low, m_up], axis=0)

    lane_shift = (HEAD_DIM - head) % HEAD_DIM

    def cum_body(step, c):
        tiles = [step * CUM_TILES + j for j in range(CUM_TILES)]
        rows = [pl.ds(pl.multiple_of(i * PAIR, PAIR), PAIR) for i in tiles]
        xs = [pltpu.roll(gate_ref[r, :], lane_shift, 1) for r in rows]
        his = [x.astype(BF16) for x in xs]
        r1s = [x - hi.astype(F32) for x, hi in zip(xs, his)]
        mids = [r1.astype(BF16) for r1 in r1s]
        los = [(r1 - mid.astype(F32)).astype(BF16) for r1, mid in zip(r1s, mids)]
        cums = [_dot(m_cum, jnp.concatenate(parts, axis=1)) for parts in zip(his, mids, los)]
        bwd_lane = lax.broadcasted_iota(jnp.int32, (PAIR, HEAD_DIM), 1) == LANE_G[1]
        for i, r, x, cum in zip(tiles, rows, xs, cums):
            cum = cum[:, :HEAD_DIM] + cum[:, HEAD_DIM:2 * HEAD_DIM] + cum[:, 2 * HEAD_DIM:]
            pre, suf = cum[:PAIR], cum[PAIR:]
            gc = jnp.where(bwd_lane, suf, pre)
            gsel_ref[r, :] = x
            gc_ref[r, :] = gc
            gx_ref[r, :] = jnp.where(bwd_lane, pre, suf) - x
            gct = gc.T
            grow_ref[i, 0:1, :] = gct[LANE_G[0]:LANE_G[0] + 1, :]
            grow_ref[i, 1:2, :] = gct[LANE_G[1]:LANE_G[1] + 1, :]
        return c

    lax.fori_loop(0, n_pairs // CUM_TILES, cum_body, 0)

    def oacc_zero(i, c):
        r = pl.multiple_of(i * CONV_TILE, CONV_TILE)
        oacc_ref[pl.ds(r, CONV_TILE), :] = jnp.zeros((CONV_TILE, HEAD_DIM), F32)
        return c

    lax.fori_loop(0, n_tiles, oacc_zero, 0)

    def window_step(step):
        rw = lax.broadcasted_iota(jnp.int32, (WIN, WIN), 0)
        cw = lax.broadcasted_iota(jnp.int32, (WIN, WIN), 1)
        eye = jnp.where(rw == cw, 1.0, 0.0)

        def same_block(n):
            return (rw // n) == (cw // n)

        chains = []
        for wi in range(WINDOWS_PER_STEP):
            for d in range(2):
                w = step * WINDOWS_PER_STEP + wi
                if d == 1:
                    w = n_win - 1 - w
                rows = pl.ds(pl.multiple_of(w * WIN, WIN), WIN)
                q = qs_ref[rows, :]
                k = ks_ref[rows, :]
                v = vs_ref[rows, :]
                gs = gsel_ref[rows, :]
                gcv = gc_ref[rows, :]
                gxv = gx_ref[rows, :]
                growt = jnp.concatenate([grow_ref[2 * w], grow_ref[2 * w + 1]], axis=1)
                kbf = k.astype(BF16)
                beta = gs[:, LANE_BETA[d]:LANE_BETA[d] + 1]
                gcol = gcv[:, LANE_G[d]:LANE_G[d] + 1]
                eg = jnp.exp(gcol)
                kb = k * beta
                incl = same_block(CHUNK) & ((rw >= cw) if d == 0 else (rw <= cw))
                decay = jnp.where(incl, jnp.exp(jnp.where(incl, gcol - growt[d:d + 1, :], 0.0)), 0.0)
                gram = _dot_nt(jnp.concatenate([kb, q], axis=0).astype(BF16), kbf)
                chains.append(dict(
                    w=w, d=d, rows=rows, gcol=gcol,
                    rhs=jnp.concatenate([v * beta, kb * eg], axis=1).astype(BF16),
                    qd=q * eg,
                    kd=k * jnp.exp(gxv[:, LANE_G[d]:LANE_G[d] + 1]),
                    lmat=jnp.where(rw != cw, gram[:WIN] * decay, 0.0),
                    amat=(gram[WIN:] * decay).astype(BF16)))

        base = same_block(BASE_BLOCK)
        l4 = [jnp.where(base, ch["lmat"], 0.0) for ch in chains]
        l4b = [x.astype(BF16) for x in l4]
        sq = [_dot(x, x) for x in l4b]
        dinv = [(eye - x) + _dot((eye - x).astype(BF16), s.astype(BF16)) for x, s in zip(l4, sq)]
        n = BASE_BLOCK
        while n < CHUNK:
            off = same_block(2 * n) & jnp.logical_not(same_block(n))
            lo = [jnp.where(off, ch["lmat"], 0.0).astype(BF16) for ch in chains]
            db = [x.astype(BF16) for x in dinv]
            if n % 8 == 0:
                g = WIN // (2 * n)
                split = [x.reshape(g, 2 * n, WIN) for x in dinv]
                fwd = [ch["d"] == 0 for ch in chains]
                moving = [(s[:, n:, :] if f else s[:, :n, :]).reshape(WIN // 2, WIN) for s, f in zip(split, fwd)]
                xs = [_dot(a.astype(BF16), b) for a, b in zip(moving, lo)]
                moved = [(a - _dot(x.astype(BF16), b)).reshape(g, n, WIN) for a, x, b in zip(moving, xs, db)]
                dinv = [jnp.concatenate([s[:, :n, :], mv] if f else [mv, s[:, n:, :]], axis=1).reshape(WIN, WIN)
                        for s, mv, f in zip(split, moved, fwd)]
            else:
                xs = [_dot(a, b) for a, b in zip(db, lo)]
                dinv = [a - _dot(x.astype(BF16), b) for a, x, b in zip(dinv, xs, db)]
            n *= 2

        colchunk = lax.broadcasted_iota(jnp.int32, (HEAD_DIM, WIN), 1) // CHUNK
        uws = [_dot(a.astype(BF16), ch["rhs"]).astype(BF16) for a, ch in zip(dinv, chains)]
        auws = [_dot(ch["amat"], uw) for ch, uw in zip(chains, uws)]
        mns = []
        for ch, uw in zip(chains, uws):
            kd = ch["kd"]
            kdt = jnp.concatenate([kd[:PAIR].T, kd[PAIR:].T], axis=1)
            lhs4 = jnp.concatenate([jnp.where(colchunk == j, kdt, 0.0) for j in range(WIN // CHUNK)],
                                   axis=0).astype(BF16)
            mns.append(_dot(lhs4, uw))
        for ch, auw, mn in zip(chains, auws, mns):
            d = ch["d"]
            oacc_ref[ch["rows"], :] += auw[:, :HEAD_DIM]
            qp = (ch["qd"] - auw[:, HEAD_DIM:]).astype(BF16)
            for j in range(WIN // CHUNK):
                cidx = (WIN // CHUNK) * ch["w"] + j
                it = cidx if d == 0 else n_chunks - 1 - cidx
                blk = mn[j * HEAD_DIM:(j + 1) * HEAD_DIM]
                lo_row = d * LHS_ROWS
                lhs_ref[it, lo_row:lo_row + HEAD_DIM, :] = (-blk[:, HEAD_DIM:]).astype(BF16)
                lhs_ref[it, lo_row + HEAD_DIM:lo_row + LHS_ROWS, :] = qp[j * CHUNK:(j + 1) * CHUNK]
                nb_ref[it, :, d * HEAD_DIM:(d + 1) * HEAD_DIM] = blk[:, :HEAD_DIM]
                last = j * CHUNK + (CHUNK - 1 if d == 0 else 0)
                gl_ref[it, :, d * HEAD_DIM:(d + 1) * HEAD_DIM] = jnp.broadcast_to(
                    jnp.exp(ch["gcol"][last:last + 1, :]), (8, HEAD_DIM))

    def sweep_step(it, s):
        rf = pl.multiple_of(it * CHUNK, CHUNK)
        rb = pl.multiple_of((n_chunks - 1 - it) * CHUNK, CHUNK)
        r = _dot(lhs_ref[it], s.astype(BF16))
        upd = jnp.concatenate([r[0:HEAD_DIM, 0:HEAD_DIM],
                               r[LHS_ROWS:LHS_ROWS + HEAD_DIM, HEAD_DIM:]], axis=1)
        oacc_ref[pl.ds(rf, CHUNK), :] += r[HEAD_DIM:LHS_ROWS, 0:HEAD_DIM]
        oacc_ref[pl.ds(rb, CHUNK), :] += r[LHS_ROWS + HEAD_DIM:, HEAD_DIM:]
        return s * gl_ref[it][0:1, :] + upd + nb_ref[it]

    def sweep_block(step, s):
        for j in range(STEPS_PER_WINDOW_STEP):
            s = sweep_step(step * STEPS_PER_WINDOW_STEP + j, s)
        return s

    n_wsteps = n_win // WINDOWS_PER_STEP
    window_step(0)

    def pipelined_body(step, s):
        s = sweep_block(step - 1, s)
        window_step(step)
        return s

    state = lax.fori_loop(1, n_wsteps, pipelined_body, jnp.zeros((HEAD_DIM, 2 * HEAD_DIM), F32))
    sweep_block(n_wsteps - 1, state)

    nw = nw_ref[...]

    def out_body(i, c):
        r = pl.multiple_of(i * CONV_TILE, CONV_TILE)
        o = oacc_ref[pl.ds(r, CONV_TILE), :]
        y = o * lax.rsqrt(jnp.mean(o * o, axis=-1, keepdims=True) + EPS) * nw
        z = z_ref[pl.ds(r, CONV_TILE), :].astype(F32)
        o_ref[pl.ds(r, CONV_TILE), :] = (y * _silu(z)).astype(o_ref.dtype)
        return c

    lax.fori_loop(0, n_tiles, out_body, 0)


def _gdn(p_main, gates, conv_w, norm_w):
    b, t, _ = p_main.shape
    h = GDN_HEADS
    n_chunks = t // CHUNK
    n_pairs = t // PAIR
    col = lambda off: (lambda bi, hi: (bi, 0, off + hi))
    cwcol = lambda off: (lambda bi, hi: (0, off + hi))
    const2 = lambda bi, hi: (0, 0)
    return pl.pallas_call(
        _gdn_kernel,
        grid=(b, h),
        in_specs=[
            pl.BlockSpec((None, t, HEAD_DIM), col(0)),
            pl.BlockSpec((None, t, HEAD_DIM), col(h)),
            pl.BlockSpec((None, t, HEAD_DIM), col(2 * h)),
            pl.BlockSpec((None, t, HEAD_DIM), col(3 * h)),
            pl.BlockSpec((None, t, HEAD_DIM), lambda bi, hi: (bi, 0, 0)),
            pl.BlockSpec((CONV_K, HEAD_DIM), cwcol(0)),
            pl.BlockSpec((CONV_K, HEAD_DIM), cwcol(h)),
            pl.BlockSpec((CONV_K, HEAD_DIM), cwcol(2 * h)),
            pl.BlockSpec((1, HEAD_DIM), const2),
        ],
        out_specs=pl.BlockSpec((None, t, HEAD_DIM), lambda bi, hi: (bi, 0, hi)),
        out_shape=jax.ShapeDtypeStruct((b, t, h * HEAD_DIM), BF16),
        scratch_shapes=[
            pltpu.VMEM((t + 16, HEAD_DIM), F32),
            pltpu.VMEM((t, HEAD_DIM), F32),
            pltpu.VMEM((t, HEAD_DIM), F32),
            pltpu.VMEM((t, HEAD_DIM), F32),
            pltpu.VMEM((t, HEAD_DIM), F32),
            pltpu.VMEM((t, HEAD_DIM), F32),
            pltpu.VMEM((t, HEAD_DIM), F32),
            pltpu.VMEM((n_pairs, 8, HEAD_DIM), F32),
            pltpu.VMEM((n_chunks, 2 * LHS_ROWS, HEAD_DIM), BF16),
            pltpu.VMEM((n_chunks, HEAD_DIM, 2 * HEAD_DIM), F32),
            pltpu.VMEM((n_chunks, 8, 2 * HEAD_DIM), F32),
            pltpu.VMEM((t, HEAD_DIM), F32),
        ],
        compiler_params=_params(("arbitrary", "arbitrary")),
    )(p_main, p_main, p_main, p_main, gates, conv_w, conv_w, conv_w, norm_w)


def _attn_prep_kernel(q_ref, kv_ref, qw_ref, kw_ref, cos_ref, sa_ref, sb_ref, qo_ref, ko_ref):
    cos = cos_ref[...]
    sa = sa_ref[...]
    sb = sb_ref[...]

    def norm_rope(x, w):
        y = x * lax.rsqrt(jnp.mean(x * x, axis=-1, keepdims=True) + EPS) * w
        return y * cos + pltpu.roll(y, HEAD_DIM - ROT_HALF // 2, 1) * sa + pltpu.roll(y, ROT_HALF // 2, 1) * sb

    qw = qw_ref[...]
    kw = kw_ref[...]
    scale = LOG2_E * HEAD_DIM ** -0.5
    for hh in range(ATT_HEADS):
        sl = slice(hh * HEAD_DIM, (hh + 1) * HEAD_DIM)
        qo_ref[:, sl] = (norm_rope(q_ref[:, sl].astype(F32), qw) * scale).astype(qo_ref.dtype)
    for hh in range(ATT_KV_HEADS):
        sl = slice(hh * HEAD_DIM, (hh + 1) * HEAD_DIM)
        ko_ref[:, sl] = norm_rope(kv_ref[:, sl].astype(F32), kw).astype(ko_ref.dtype)


def _attn_prep(p_main, q_off, qw, kw, cos, sa, sb, tq=256):
    b, t, _ = p_main.shape
    qwid = ATT_HEADS * HEAD_DIM
    kvwid = 2 * ATT_KV_HEADS * HEAD_DIM
    row = lambda bi, i: (i, 0)
    const2 = lambda bi, i: (0, 0)
    return pl.pallas_call(
        _attn_prep_kernel,
        grid=(b, t // tq),
        in_specs=[
            pl.BlockSpec((None, tq, qwid), lambda bi, i: (bi, i, q_off // qwid)),
            pl.BlockSpec((None, tq, kvwid), lambda bi, i: (bi, i, (q_off + qwid) // kvwid)),
            pl.BlockSpec((1, HEAD_DIM), const2),
            pl.BlockSpec((1, HEAD_DIM), const2),
            pl.BlockSpec((tq, HEAD_DIM), row),
            pl.BlockSpec((tq, HEAD_DIM), row),
            pl.BlockSpec((tq, HEAD_DIM), row),
        ],
        out_specs=[
            pl.BlockSpec((None, tq, qwid), lambda bi, i: (bi, i, 0)),
            pl.BlockSpec((None, tq, ATT_KV_HEADS * HEAD_DIM), lambda bi, i: (bi, i, 0)),
        ],
        out_shape=[
            jax.ShapeDtypeStruct((b, t, qwid), BF16),
            jax.ShapeDtypeStruct((b, t, ATT_KV_HEADS * HEAD_DIM), BF16),
        ],
        compiler_params=_params(("arbitrary", "arbitrary")),
    )(p_main, p_main, qw, kw, cos, sa, sb)


def _attn_kernel(q_ref, k_ref, v_ref, o_ref, m_ref, acc_ref):
    j = pl.program_id(2)
    tq = q_ref.shape[0]
    tk = k_ref.shape[0]
    kv = range(ATT_KV_HEADS)

    @pl.when(j == 0)
    def _():
        m_ref[...] = jnp.full(m_ref.shape, jnp.finfo(F32).min, F32)
        acc_ref[...] = jnp.zeros(acc_ref.shape, F32)

    def head_cols(ref, hh):
        return ref[:, hh * HEAD_DIM:(hh + 1) * HEAD_DIM]

    ones = jnp.ones((tk, HEAD_DIM), BF16)
    qs = [jnp.concatenate([head_cols(q_ref, hk * ATT_GROUPS + g) for g in range(ATT_GROUPS)], axis=0) for hk in kv]
    vs = [jnp.concatenate([head_cols(v_ref, hk), ones], axis=1) for hk in kv]
    ss = [_dot_nt(qs[hk], head_cols(k_ref, hk)) for hk in kv]
    m_prev = [m_ref[hk] for hk in kv]
    m_new = [jnp.maximum(m_prev[hk], jnp.max(ss[hk], axis=-1, keepdims=True)) for hk in kv]
    alpha = [jnp.exp2(m_prev[hk] - m_new[hk]) for hk in kv]
    ps = [jnp.exp2(ss[hk] - m_new[hk][:, 0:1]).astype(BF16) for hk in kv]
    pv = [_dot(ps[hk], vs[hk]) for hk in kv]
    for hk in kv:
        acc_ref[hk] = jnp.concatenate([alpha[hk], alpha[hk]], axis=1) * acc_ref[hk] + pv[hk]
        m_ref[hk] = m_new[hk]

    @pl.when(j == pl.num_programs(2) - 1)
    def _():
        for hk in kv:
            acc = acc_ref[hk]
            o = acc[:, :HEAD_DIM] / acc[:, HEAD_DIM:]
            for g in range(ATT_GROUPS):
                hh = hk * ATT_GROUPS + g
                o_ref[:, hh * HEAD_DIM:(hh + 1) * HEAD_DIM] = o[g * tq:(g + 1) * tq, :].astype(o_ref.dtype)


def _attention(q_rot, k_rot, p_main, v_off, tq=256, tk=1024):
    b, t, qw = q_rot.shape
    tk = min(tk, t)
    kvw = ATT_KV_HEADS * HEAD_DIM
    rows = ATT_GROUPS * tq
    return pl.pallas_call(
        _attn_kernel,
        grid=(b, t // tq, t // tk),
        in_specs=[
            pl.BlockSpec((None, tq, qw), lambda bi, i, j: (bi, i, 0)),
            pl.BlockSpec((None, tk, kvw), lambda bi, i, j: (bi, j, 0)),
            pl.BlockSpec((None, tk, kvw), lambda bi, i, j: (bi, j, v_off // kvw)),
        ],
        out_specs=pl.BlockSpec((None, tq, qw), lambda bi, i, j: (bi, i, 0)),
        out_shape=jax.ShapeDtypeStruct((b, t, qw), BF16),
        scratch_shapes=[
            pltpu.VMEM((ATT_KV_HEADS, rows, HEAD_DIM), F32),
            pltpu.VMEM((ATT_KV_HEADS, rows, 2 * HEAD_DIM), F32),
        ],
        compiler_params=_params(("arbitrary", "arbitrary", "arbitrary")),
    )(q_rot, k_rot, p_main)


def _out_proj_kernel(ya_ref, yb_ref, wa_ref, wb_ref, x_ref, gt_ref, o_ref):
    mix = _dot(ya_ref[...], wa_ref[...]) + _dot(yb_ref[...], wb_ref[...])
    o_ref[...] = x_ref[...] + gt_ref[...] * mix


def _out_proj(ya, yb, w_out, x, gt, tm=512):
    b, t, d = x.shape
    ka = ya.shape[2]
    kb = yb.shape[2]
    return pl.pallas_call(
        _out_proj_kernel,
        grid=(b, t // tm),
        in_specs=[
            pl.BlockSpec((None, tm, ka), lambda bi, i: (bi, i, 0)),
            pl.BlockSpec((None, tm, kb), lambda bi, i: (bi, i, 0)),
            pl.BlockSpec((ka, d), lambda bi, i: (0, 0)),
            pl.BlockSpec((kb, d), lambda bi, i: (ka // kb, 0)),
            pl.BlockSpec((None, tm, d), lambda bi, i: (bi, i, 0)),
            pl.BlockSpec((None, 1, d), lambda bi, i: (bi, 0, 0)),
        ],
        out_specs=pl.BlockSpec((None, tm, d), lambda bi, i: (bi, i, 0)),
        out_shape=jax.ShapeDtypeStruct((b, t, d), F32),
        compiler_params=_params(("arbitrary", "arbitrary")),
    )(ya, yb, w_out, w_out, x, gt)


def _mlp_kernel(x_ref, nw_ref, sc_ref, sh_ref, gt_ref, wu_ref, wd_ref, o_ref, h_ref, acc_ref):
    f = pl.program_id(2)

    @pl.when(f == 0)
    def _():
        h_ref[...] = _modulated_norm(x_ref[...], nw_ref[...], sc_ref[...], sh_ref[...]).astype(BF16)
        acc_ref[...] = jnp.zeros(acc_ref.shape, F32)

    u = jnp.maximum(_dot(h_ref[...], wu_ref[...]), 0.0)
    acc_ref[...] += _dot((u * u).astype(BF16), wd_ref[...])

    @pl.when(f == pl.num_programs(2) - 1)
    def _():
        o_ref[...] = x_ref[...] + gt_ref[...] * acc_ref[...]


def _mlp(x, nw, sc, sh, gt, w_up, w_down, tm=512, tf=1024):
    b, t, d = x.shape
    dff = w_up.shape[1]
    vec = lambda bi, i, f: (bi, 0, 0)
    return pl.pallas_call(
        _mlp_kernel,
        grid=(b, t // tm, dff // tf),
        in_specs=[
            pl.BlockSpec((None, tm, d), lambda bi, i, f: (bi, i, 0)),
            pl.BlockSpec((1, d), lambda bi, i, f: (0, 0)),
            pl.BlockSpec((None, 1, d), vec),
            pl.BlockSpec((None, 1, d), vec),
            pl.BlockSpec((None, 1, d), vec),
            pl.BlockSpec((d, tf), lambda bi, i, f: (0, f)),
            pl.BlockSpec((tf, d), lambda bi, i, f: (f, 0)),
        ],
        out_specs=pl.BlockSpec((None, tm, d), lambda bi, i, f: (bi, i, 0)),
        out_shape=jax.ShapeDtypeStruct((b, t, d), F32),
        scratch_shapes=[pltpu.VMEM((tm, d), BF16), pltpu.VMEM((tm, d), F32)],
        compiler_params=_params(("arbitrary", "arbitrary", "arbitrary")),
    )(x, nw, sc, sh, gt, w_up, w_down)


def _rope_tables(t):
    rows = t // GRID_W
    row = jnp.repeat(jnp.arange(rows, dtype=F32), GRID_W)
    col = jnp.tile(jnp.arange(GRID_W, dtype=F32), rows)
    inv_freq = ROPE_THETA ** (-jnp.arange(0, ROT_HALF, 2, dtype=F32) / ROT_HALF)
    ang_r = row[:, None] * inv_freq[None, :]
    ang_c = col[:, None] * inv_freq[None, :]
    ang = jnp.concatenate([ang_r, ang_r, ang_c, ang_c], axis=-1)
    cos, sin = jnp.cos(ang), jnp.sin(ang)
    first = (jnp.arange(HEAD_DIM) % ROT_HALF) < ROT_HALF // 2
    return cos, jnp.where(first, -sin, 0.0), jnp.where(first, 0.0, sin)


def _gate_rows(a_log, dt_bias):
    def place(p):
        return jnp.pad(p.reshape(1, 2 * GDN_HEADS), ((0, 0), (2 * GDN_HEADS, HEAD_DIM - 4 * GDN_HEADS)))
    return place(a_log.astype(F32)), place(dt_bias.astype(F32))


def _trunk_layer(x, mod, norm1_w, norm2_w, w_main, w_ba, conv_w, alog_row, dt_row, gdn_norm_w,
                 q_norm_w, k_norm_w, w_out, w_up, w_down, rope):
    b, t, d = x.shape
    sh1, sc1, gt1, sh2, sc2, gt2 = [m.reshape(b, 1, d) for m in jnp.split(mod, N_MOD, axis=-1)]
    gdn_w = GDN_HEADS * HEAD_DIM
    p_main, gates = _in_proj(x, norm1_w, sc1, sh1, w_main, w_ba, alog_row, dt_row)
    ya = _gdn(p_main, gates, conv_w, gdn_norm_w)
    att_off = 4 * gdn_w
    q_rot, k_rot = _attn_prep(p_main, att_off, q_norm_w, k_norm_w, *rope)
    v_off = att_off + (ATT_HEADS + ATT_KV_HEADS) * HEAD_DIM
    yb = _attention(q_rot, k_rot, p_main, v_off)
    x = _out_proj(ya, yb, w_out, x, gt1)
    return _mlp(x, norm2_w, sc2, sh2, gt2, w_up, w_down)


def kernel(x_prompt, x_sample, c_prompt, c_sample, ada_w, ada_b, norm1_w, norm2_w, w_in, conv_w, a_log,
           dt_bias, gdn_norm_w, q_norm_w, k_norm_w, w_out, w_up, w_down):
    depth = ada_w.shape[0]
    gdn_w = GDN_HEADS * HEAD_DIM
    n_gate = 4 * GDN_HEADS
    nb_prompt = c_prompt.shape[0]

    mod = _adaln(jnp.concatenate([c_prompt, c_sample], axis=0), ada_w, ada_b)

    w_main = jnp.concatenate([w_in[:, :, :4 * gdn_w], w_in[:, :, 4 * gdn_w + n_gate:]], axis=-1).astype(BF16)
    w_ba = jnp.pad(w_in[:, :, 4 * gdn_w:4 * gdn_w + n_gate], ((0, 0), (0, 0), (0, HEAD_DIM - n_gate))).astype(BF16)
    w_out_b = w_out.astype(BF16)
    w_up_b = w_up.astype(BF16)
    w_down_b = w_down.astype(BF16)

    xs = [x_prompt, x_sample]
    ropes = [_rope_tables(x.shape[1]) for x in xs]
    for l in range(depth):
        alog_row, dt_row = _gate_rows(a_log[l], dt_bias[l])
        mods = [mod[l, :nb_prompt], mod[l, nb_prompt:]]
        for i in range(2):
            xs[i] = _trunk_layer(
                xs[i], mods[i], norm1_w[l].reshape(1, -1), norm2_w[l].reshape(1, -1), w_main[l], w_ba[l],
                conv_w[l], alog_row, dt_row, gdn_norm_w[l].reshape(1, -1), q_norm_w[l].reshape(1, -1),
                k_norm_w[l].reshape(1, -1), w_out_b[l], w_up_b[l], w_down_b[l], ropes[i])
    return (xs[0], xs[1])
```

```python
import functools

import jax
import jax.numpy as jnp
from jax import lax
from jax.experimental import pallas as pl
from jax.experimental.pallas import tpu as pltpu

F32 = jnp.float32
BF16 = jnp.bfloat16

HEAD_DIM = 128
GDN_HEADS = 8
ATT_HEADS = 8
ATT_KV_HEADS = 2
ATT_GROUPS = ATT_HEADS // ATT_KV_HEADS
CONV_K = 5
CHUNK = 64
PAIR = 2 * CHUNK
WIN = 4 * CHUNK
LHS_ROWS = HEAD_DIM + CHUNK
GRID_W = 64
ROPE_THETA = 10000.0
ROT_HALF = HEAD_DIM // 2
N_MOD = 6
EPS = 1e-6
LOG2_E = 1.4426950408889634
BASE_BLOCK = 8
WINDOWS_PER_STEP = 2
STEPS_PER_WINDOW_STEP = WINDOWS_PER_STEP * (WIN // CHUNK)
RING = 2 * STEPS_PER_WINDOW_STEP

VMEM_LIMIT = 56 * 1024 * 1024


def _sigmoid(x):
    return 1.0 / (1.0 + jnp.exp(-x))


def _silu(x):
    return x * _sigmoid(x)


def _softplus(x):
    return jnp.maximum(x, 0.0) + jnp.log(1.0 + jnp.exp(-jnp.abs(x)))


def _dot(a, b):
    return jnp.dot(a, b, preferred_element_type=F32)


def _dot_nt(a, b):
    return lax.dot_general(a, b, (((1,), (1,)), ((), ())), preferred_element_type=F32)


def _params(sem):
    return pltpu.CompilerParams(dimension_semantics=sem, vmem_limit_bytes=VMEM_LIMIT)


def _adaln_kernel(c_ref, w_ref, b_ref, o_ref):
    s = _silu(c_ref[...]).astype(BF16)
    o_ref[...] = _dot(s, w_ref[...].astype(BF16)) + b_ref[...]


def _adaln(c, ada_w, ada_b, tn=1024):
    depth, d, n = ada_w.shape
    rows = c.shape[0]
    return pl.pallas_call(
        _adaln_kernel,
        grid=(depth, n // tn),
        in_specs=[
            pl.BlockSpec((rows, d), lambda l, j: (0, 0)),
            pl.BlockSpec((None, d, tn), lambda l, j: (l, 0, j)),
            pl.BlockSpec((None, 1, tn), lambda l, j: (l, 0, j)),
        ],
        out_specs=pl.BlockSpec((None, rows, tn), lambda l, j: (l, 0, j)),
        out_shape=jax.ShapeDtypeStruct((depth, rows, n), F32),
        compiler_params=_params(("arbitrary", "arbitrary")),
    )(c, ada_w, ada_b.reshape(depth, 1, n))


def _modulated_norm(x, nw, sc, sh):
    y = x * lax.rsqrt(jnp.mean(x * x, axis=-1, keepdims=True) + EPS) * nw
    return y * (1.0 + sc) + sh


def _in_proj_kernel(x_ref, nw_ref, sc_ref, sh_ref, w_ref, wba_ref, alog_ref, dt_ref, o_ref, gate_ref, h_ref):
    @pl.when(pl.program_id(2) == 0)
    def _():
        h = _modulated_norm(x_ref[...], nw_ref[...], sc_ref[...], sh_ref[...]).astype(BF16)
        h_ref[...] = h
        ba = _dot(h, wba_ref[...])
        lane = lax.broadcasted_iota(jnp.int32, ba.shape, 1)
        gate_ref[...] = jnp.where(lane < 2 * GDN_HEADS, _sigmoid(ba),
                                  -jnp.exp(alog_ref[...]) * _softplus(ba + dt_ref[...]))

    o_ref[...] = _dot(h_ref[...], w_ref[...]).astype(o_ref.dtype)


def _in_proj(x, nw, sc, sh, w_main, w_ba, layer, alog_row, dt_row, tm=1024, tn=1408):
    b, t, d = x.shape
    n = w_main.shape[2]
    nba = w_ba.shape[2]
    return pl.pallas_call(
        _in_proj_kernel,
        grid=(b, t // tm, n // tn),
        in_specs=[
            pl.BlockSpec((None, tm, d), lambda bi, i, j: (bi, i, 0)),
            pl.BlockSpec((1, d), lambda bi, i, j: (0, 0)),
            pl.BlockSpec((None, 1, d), lambda bi, i, j: (bi, 0, 0)),
            pl.BlockSpec((None, 1, d), lambda bi, i, j: (bi, 0, 0)),
            pl.BlockSpec((None, d, tn), lambda bi, i, j: (layer, 0, j)),
            pl.BlockSpec((None, d, nba), lambda bi, i, j: (layer, 0, 0)),
            pl.BlockSpec((1, nba), lambda bi, i, j: (0, 0)),
            pl.BlockSpec((1, nba), lambda bi, i, j: (0, 0)),
        ],
        out_specs=[
            pl.BlockSpec((None, tm, tn), lambda bi, i, j: (bi, i, j)),
            pl.BlockSpec((None, tm, nba), lambda bi, i, j: (bi, i, 0)),
        ],
        out_shape=[
            jax.ShapeDtypeStruct((b, t, n), BF16),
            jax.ShapeDtypeStruct((b, t, nba), F32),
        ],
        scratch_shapes=[pltpu.VMEM((tm, d), BF16)],
        compiler_params=_params(("arbitrary", "arbitrary", "arbitrary")),
    )(x, nw, sc, sh, w_main, w_ba, alog_row, dt_row)


CONV_TILE = 256
CUM_TILES = 4
LANE_BETA = (0, 8)
LANE_G = (16, 24)


def _gdn_kernel(q_ref, k_ref, v_ref, z_ref, gate_ref, cwq_ref, cwk_ref, cwv_ref, nw_ref,
                o_ref,
                pad_ref, qs_ref, ks_ref, vs_ref, gsel_ref, gc_ref, gx_ref, grow_ref,
                lhs_ref, nb_ref, gl_ref, oacc_ref):
    t = q_ref.shape[0]
    n_tiles = t // CONV_TILE
    n_pairs = t // PAIR
    n_chunks = t // CHUNK
    n_win = t // WIN
    head = pl.program_id(1)

    halo = 8
    pad_ref[0:halo, :] = jnp.zeros((halo, HEAD_DIM), F32)
    pad_ref[halo + t:halo + t + halo, :] = jnp.zeros((halo, HEAD_DIM), F32)

    def conv_silu(src_ref, cw_ref, dst_ref, normalize, scale):
        def fill(i, c):
            r = pl.multiple_of(i * CONV_TILE, CONV_TILE)
            pad_ref[pl.ds(halo + r, CONV_TILE), :] = src_ref[pl.ds(r, CONV_TILE), :].astype(F32)
            return c

        lax.fori_loop(0, n_tiles, fill, 0)
        cw = cw_ref[...]

        def body(i, c):
            starts = [pl.multiple_of((2 * i + j) * CONV_TILE, CONV_TILE) for j in range(2)]
            accs = [None, None]
            for j in range(CONV_K):
                off = halo - CONV_K // 2 + j
                for n, r in enumerate(starts):
                    term = pad_ref[pl.ds(r + off, CONV_TILE), :] * cw[j:j + 1, :]
                    accs[n] = term if accs[n] is None else accs[n] + term
            ys = [_silu(acc) for acc in accs]
            if normalize:
                norms = [lax.rsqrt(jnp.sum(y * y, axis=-1, keepdims=True) + EPS) for y in ys]
                ys = [y * nrm for y, nrm in zip(ys, norms)]
            if scale != 1.0:
                ys = [y * scale for y in ys]
            for r, y in zip(starts, ys):
                dst_ref[pl.ds(r, CONV_TILE), :] = y
            return c

        lax.fori_loop(0, n_tiles // 2, body, 0)

    conv_silu(q_ref, cwq_ref, qs_ref, True, HEAD_DIM ** -0.5)
    conv_silu(k_ref, cwk_ref, ks_ref, True, 1.0)
    conv_silu(v_ref, cwv_ref, vs_ref, False, 1.0)

    ri = lax.broadcasted_iota(jnp.int32, (PAIR, PAIR), 0)
    ci = lax.broadcasted_iota(jnp.int32, (PAIR, PAIR), 1)
    same = (ri // CHUNK) == (ci // CHUNK)
    m_low = jnp.where(same & (ci <= ri), 1.0, 0.0).astype(BF16)
    m_up = jnp.where(same & (ci >= ri), 1.0, 0.0).astype(BF16)
    m_cum = jnp.concatenate([m_low, m_up], axis=0)

    lane_shift = (HEAD_DIM - head) % HEAD_DIM

    def cum_body(step, c):
        tiles = [step * CUM_TILES + j for j in range(CUM_TILES)]
        rows = [pl.ds(pl.multiple_of(i * PAIR, PAIR), PAIR) for i in tiles]
        xs = [pltpu.roll(gate_ref[r, :], lane_shift, 1) for r in rows]
        his = [x.astype(BF16) for x in xs]
        r1s = [x - hi.astype(F32) for x, hi in zip(xs, his)]
        mids = [r1.astype(BF16) for r1 in r1s]
        los = [(r1 - mid.astype(F32)).astype(BF16) for r1, mid in zip(r1s, mids)]
        cums = [_dot(m_cum, jnp.concatenate(parts, axis=1)) for parts in zip(his, mids, los)]
        bwd_lane = lax.broadcasted_iota(jnp.int32, (PAIR, HEAD_DIM), 1) == LANE_G[1]
        for i, r, x, cum in zip(tiles, rows, xs, cums):
            cum = cum[:, :HEAD_DIM] + cum[:, HEAD_DIM:2 * HEAD_DIM] + cum[:, 2 * HEAD_DIM:]
            pre, suf = cum[:PAIR], cum[PAIR:]
            gc = jnp.where(bwd_lane, suf, pre)
            gsel_ref[r, :] = x
            gc_ref[r, :] = gc
            gx_ref[r, :] = jnp.where(bwd_lane, pre, suf) - x
            gct = gc.T
            grow_ref[i, 0:1, :] = gct[LANE_G[0]:LANE_G[0] + 1, :]
            grow_ref[i, 1:2, :] = gct[LANE_G[1]:LANE_G[1] + 1, :]
        return c

    lax.fori_loop(0, n_pairs // CUM_TILES, cum_body, 0)

    def oacc_zero(i, c):
        r = pl.multiple_of(i * CONV_TILE, CONV_TILE)
        oacc_ref[pl.ds(r, CONV_TILE), :] = jnp.zeros((CONV_TILE, HEAD_DIM), F32)
        return c

    lax.fori_loop(0, n_tiles, oacc_zero, 0)

    def window_step(step):
        rw = lax.broadcasted_iota(jnp.int32, (WIN, WIN), 0)
        cw = lax.broadcasted_iota(jnp.int32, (WIN, WIN), 1)
        eye = jnp.where(rw == cw, 1.0, 0.0)

        def same_block(n):
            return (rw // n) == (cw // n)

        chains = []
        for wi in range(WINDOWS_PER_STEP):
            for d in range(2):
                w = step * WINDOWS_PER_STEP + wi
                if d == 1:
                    w = n_win - 1 - w
                rows = pl.ds(pl.multiple_of(w * WIN, WIN), WIN)
                q = qs_ref[rows, :]
                k = ks_ref[rows, :]
                v = vs_ref[rows, :]
                gs = gsel_ref[rows, :]
                gcv = gc_ref[rows, :]
                gxv = gx_ref[rows, :]
                growt = jnp.concatenate([grow_ref[2 * w], grow_ref[2 * w + 1]], axis=1)
                kbf = k.astype(BF16)
                beta = gs[:, LANE_BETA[d]:LANE_BETA[d] + 1]
                gcol = gcv[:, LANE_G[d]:LANE_G[d] + 1]
                eg = jnp.exp(gcol)
                kb = k * beta
                incl = same_block(CHUNK) & ((rw >= cw) if d == 0 else (rw <= cw))
                decay = jnp.where(incl, jnp.exp(jnp.where(incl, gcol - growt[d:d + 1, :], 0.0)), 0.0)
                gram = _dot_nt(jnp.concatenate([kb, q], axis=0).astype(BF16), kbf)
                chains.append(dict(
                    w=w, d=d, rows=rows, gcol=gcol,
                    rhs=jnp.concatenate([v * beta, kb * eg], axis=1).astype(BF16),
                    qd=q * eg,
                    kd=k * jnp.exp(gxv[:, LANE_G[d]:LANE_G[d] + 1]),
                    lmat=jnp.where(rw != cw, gram[:WIN] * decay, 0.0),
                    amat=(gram[WIN:] * decay).astype(BF16)))

        base = same_block(BASE_BLOCK)
        lb = [jnp.where(base, ch["lmat"], 0.0) for ch in chains]
        lbb = [x.astype(BF16) for x in lb]
        p2 = [_dot(x, x) for x in lbb]
        p2b = [x.astype(BF16) for x in p2]
        p4 = [_dot(x, x) for x in p2b]
        r1 = [(eye - x) + _dot((eye - x).astype(BF16), s) for x, s in zip(lb, p2b)]
        dinv = [r + _dot(r.astype(BF16), s.astype(BF16)) for r, s in zip(r1, p4)]
        n = BASE_BLOCK
        while n < CHUNK:
            off = same_block(2 * n) & jnp.logical_not(same_block(n))
            lo = [jnp.where(off, ch["lmat"], 0.0).astype(BF16) for ch in chains]
            db = [x.astype(BF16) for x in dinv]
            g = WIN // (2 * n)
            split = [x.reshape(g, 2 * n, WIN) for x in dinv]
            fwd = [ch["d"] == 0 for ch in chains]
            moving = [(s[:, n:, :] if f else s[:, :n, :]).reshape(WIN // 2, WIN) for s, f in zip(split, fwd)]
            xs = [_dot(a.astype(BF16), b) for a, b in zip(moving, lo)]
            moved = [(a - _dot(x.astype(BF16), b)).reshape(g, n, WIN) for a, x, b in zip(moving, xs, db)]
            dinv = [jnp.concatenate([s[:, :n, :], mv] if f else [mv, s[:, n:, :]], axis=1).reshape(WIN, WIN)
                    for s, mv, f in zip(split, moved, fwd)]
            n *= 2

        colchunk = lax.broadcasted_iota(jnp.int32, (HEAD_DIM, WIN), 1) // CHUNK
        uws = [_dot(a.astype(BF16), ch["rhs"]).astype(BF16) for a, ch in zip(dinv, chains)]
        auws = [_dot(ch["amat"], uw) for ch, uw in zip(chains, uws)]
        mns = []
        for ch, uw in zip(chains, uws):
            kd = ch["kd"]
            kdt = jnp.concatenate([kd[:PAIR].T, kd[PAIR:].T], axis=1)
            lhs4 = jnp.concatenate([jnp.where(colchunk == j, kdt, 0.0) for j in range(WIN // CHUNK)],
                                   axis=0).astype(BF16)
            mns.append(_dot(lhs4, uw))
        for ch, auw, mn in zip(chains, auws, mns):
            d = ch["d"]
            oacc_ref[ch["rows"], :] += auw[:, :HEAD_DIM]
            qp = (ch["qd"] - auw[:, HEAD_DIM:]).astype(BF16)
            for j in range(WIN // CHUNK):
                cidx = (WIN // CHUNK) * ch["w"] + j
                slot = (cidx if d == 0 else n_chunks - 1 - cidx) % RING
                blk = mn[j * HEAD_DIM:(j + 1) * HEAD_DIM]
                lo_row = d * LHS_ROWS
                lhs_ref[slot, lo_row:lo_row + HEAD_DIM, :] = (-blk[:, HEAD_DIM:]).astype(BF16)
                lhs_ref[slot, lo_row + HEAD_DIM:lo_row + LHS_ROWS, :] = qp[j * CHUNK:(j + 1) * CHUNK]
                nb_ref[slot, :, d * HEAD_DIM:(d + 1) * HEAD_DIM] = blk[:, :HEAD_DIM]
                last = j * CHUNK + (CHUNK - 1 if d == 0 else 0)
                gl_ref[slot, :, d * HEAD_DIM:(d + 1) * HEAD_DIM] = jnp.broadcast_to(
                    jnp.exp(ch["gcol"][last:last + 1, :]), (8, HEAD_DIM))

    def sweep_step(it, s):
        rf = pl.multiple_of(it * CHUNK, CHUNK)
        rb = pl.multiple_of((n_chunks - 1 - it) * CHUNK, CHUNK)
        slot = it % RING
        r = _dot(lhs_ref[slot], s.astype(BF16))
        upd = jnp.concatenate([r[0:HEAD_DIM, 0:HEAD_DIM],
                               r[LHS_ROWS:LHS_ROWS + HEAD_DIM, HEAD_DIM:]], axis=1)
        oacc_ref[pl.ds(rf, CHUNK), :] += r[HEAD_DIM:LHS_ROWS, 0:HEAD_DIM]
        oacc_ref[pl.ds(rb, CHUNK), :] += r[LHS_ROWS + HEAD_DIM:, HEAD_DIM:]
        return s * gl_ref[slot][0:1, :] + upd + nb_ref[slot]

    def sweep_block(step, s):
        for j in range(STEPS_PER_WINDOW_STEP):
            s = sweep_step(step * STEPS_PER_WINDOW_STEP + j, s)
        return s

    n_wsteps = n_win // WINDOWS_PER_STEP
    window_step(0)

    def pipelined_body(step, s):
        s = sweep_block(step - 1, s)
        window_step(step)
        return s

    state = lax.fori_loop(1, n_wsteps, pipelined_body, jnp.zeros((HEAD_DIM, 2 * HEAD_DIM), F32))
    sweep_block(n_wsteps - 1, state)

    nw = nw_ref[...]

    def out_body(i, c):
        r = pl.multiple_of(i * CONV_TILE, CONV_TILE)
        o = oacc_ref[pl.ds(r, CONV_TILE), :]
        y = o * lax.rsqrt(jnp.mean(o * o, axis=-1, keepdims=True) + EPS) * nw
        z = z_ref[pl.ds(r, CONV_TILE), :].astype(F32)
        o_ref[pl.ds(r, CONV_TILE), :] = (y * _silu(z)).astype(o_ref.dtype)
        return c

    lax.fori_loop(0, n_tiles, out_body, 0)


def _gdn(p_main, gates, conv_w, norm_w):
    b, t, _ = p_main.shape
    h = GDN_HEADS
    n_chunks = t // CHUNK
    n_pairs = t // PAIR
    col = lambda off: (lambda bi, hi: (bi, 0, off + hi))
    cwcol = lambda off: (lambda bi, hi: (0, off + hi))
    const2 = lambda bi, hi: (0, 0)
    return pl.pallas_call(
        _gdn_kernel,
        grid=(b, h),
        in_specs=[
            pl.BlockSpec((None, t, HEAD_DIM), col(0)),
            pl.BlockSpec((None, t, HEAD_DIM), col(h)),
            pl.BlockSpec((None, t, HEAD_DIM), col(2 * h)),
            pl.BlockSpec((None, t, HEAD_DIM), col(3 * h)),
            pl.BlockSpec((None, t, HEAD_DIM), lambda bi, hi: (bi, 0, 0), pipeline_mode=pl.Buffered(1)),
            pl.BlockSpec((CONV_K, HEAD_DIM), cwcol(0)),
            pl.BlockSpec((CONV_K, HEAD_DIM), cwcol(h)),
            pl.BlockSpec((CONV_K, HEAD_DIM), cwcol(2 * h)),
            pl.BlockSpec((1, HEAD_DIM), const2),
        ],
        out_specs=pl.BlockSpec((None, t, HEAD_DIM), lambda bi, hi: (bi, 0, hi)),
        out_shape=jax.ShapeDtypeStruct((b, t, h * HEAD_DIM), BF16),
        scratch_shapes=[
            pltpu.VMEM((t + 16, HEAD_DIM), F32),
            pltpu.VMEM((t, HEAD_DIM), F32),
            pltpu.VMEM((t, HEAD_DIM), F32),
            pltpu.VMEM((t, HEAD_DIM), F32),
            pltpu.VMEM((t, HEAD_DIM), F32),
            pltpu.VMEM((t, HEAD_DIM), F32),
            pltpu.VMEM((t, HEAD_DIM), F32),
            pltpu.VMEM((n_pairs, 8, HEAD_DIM), F32),
            pltpu.VMEM((RING, 2 * LHS_ROWS, HEAD_DIM), BF16),
            pltpu.VMEM((RING, HEAD_DIM, 2 * HEAD_DIM), F32),
            pltpu.VMEM((RING, 8, 2 * HEAD_DIM), F32),
            pltpu.VMEM((t, HEAD_DIM), F32),
        ],
        compiler_params=_params(("arbitrary", "arbitrary")),
    )(p_main, p_main, p_main, p_main, gates, conv_w, conv_w, conv_w, norm_w)


def _attn_prep_kernel(q_ref, kv_ref, qw_ref, kw_ref, cos_ref, sin_ref, qo_ref, ko_ref):
    cos = cos_ref[...]
    sin = sin_ref[...]
    quarter = ROT_HALF // 2
    src = lax.broadcasted_iota(jnp.int32, (HEAD_DIM, HEAD_DIM), 0)
    dst = lax.broadcasted_iota(jnp.int32, (HEAD_DIM, HEAD_DIM), 1)
    dst_first = (dst % ROT_HALF) < quarter
    perm = jnp.where(dst_first & (src == dst + quarter), -1.0,
                     jnp.where(jnp.logical_not(dst_first) & (src == dst - quarter), 1.0, 0.0)).astype(BF16)
    lane_first = (lax.broadcasted_iota(jnp.int32, (1, HEAD_DIM), 1) % ROT_HALF) < quarter

    def tables(w, scale):
        w_rot = jnp.where(lane_first, pltpu.roll(w, HEAD_DIM - quarter, 1), pltpu.roll(w, quarter, 1))
        return (w * scale) * cos, (w_rot * scale) * sin

    def norm_rope(xb, wc, ws):
        x = xb.astype(F32)
        rstd = lax.rsqrt(jnp.mean(x * x, axis=-1, keepdims=True) + EPS)
        return (x * wc + _dot(xb, perm) * ws) * rstd

    q_tabs = tables(qw_ref[...], LOG2_E * HEAD_DIM ** -0.5)
    k_tabs = tables(kw_ref[...], 1.0)
    for hh in range(ATT_HEADS):
        sl = slice(hh * HEAD_DIM, (hh + 1) * HEAD_DIM)
        qo_ref[:, sl] = norm_rope(q_ref[:, sl], *q_tabs).astype(qo_ref.dtype)
    for hh in range(ATT_KV_HEADS):
        sl = slice(hh * HEAD_DIM, (hh + 1) * HEAD_DIM)
        ko_ref[:, sl] = norm_rope(kv_ref[:, sl], *k_tabs).astype(ko_ref.dtype)


def _attn_prep(p_main, q_off, qw, kw, cos, sin, tq=256):
    b, t, _ = p_main.shape
    qwid = ATT_HEADS * HEAD_DIM
    kvwid = 2 * ATT_KV_HEADS * HEAD_DIM
    row = lambda bi, i: (i, 0)
    const2 = lambda bi, i: (0, 0)
    return pl.pallas_call(
        _attn_prep_kernel,
        grid=(b, t // tq),
        in_specs=[
            pl.BlockSpec((None, tq, qwid), lambda bi, i: (bi, i, q_off // qwid)),
            pl.BlockSpec((None, tq, kvwid), lambda bi, i: (bi, i, (q_off + qwid) // kvwid)),
            pl.BlockSpec((1, HEAD_DIM), const2),
            pl.BlockSpec((1, HEAD_DIM), const2),
            pl.BlockSpec((tq, HEAD_DIM), row),
            pl.BlockSpec((tq, HEAD_DIM), row),
        ],
        out_specs=[
            pl.BlockSpec((None, tq, qwid), lambda bi, i: (bi, i, 0)),
            pl.BlockSpec((None, tq, ATT_KV_HEADS * HEAD_DIM), lambda bi, i: (bi, i, 0)),
        ],
        out_shape=[
            jax.ShapeDtypeStruct((b, t, qwid), BF16),
            jax.ShapeDtypeStruct((b, t, ATT_KV_HEADS * HEAD_DIM), BF16),
        ],
        compiler_params=_params(("arbitrary", "arbitrary")),
    )(p_main, p_main, qw, kw, cos, sin)


def _attn_kernel(q_ref, k_ref, v_ref, o_ref, m_ref, acc_ref):
    j = pl.program_id(2)
    tq = q_ref.shape[0]
    tk = k_ref.shape[0]
    kv = range(ATT_KV_HEADS)

    @pl.when(j == 0)
    def _():
        m_ref[...] = jnp.full(m_ref.shape, jnp.finfo(F32).min, F32)
        acc_ref[...] = jnp.zeros(acc_ref.shape, F32)

    def head_cols(ref, hh):
        return ref[:, hh * HEAD_DIM:(hh + 1) * HEAD_DIM]

    ones = jnp.ones((tk, HEAD_DIM), BF16)
    qs = [jnp.concatenate([head_cols(q_ref, hk * ATT_GROUPS + g) for g in range(ATT_GROUPS)], axis=0) for hk in kv]
    vs = [jnp.concatenate([head_cols(v_ref, hk), ones], axis=1) for hk in kv]
    ss = [_dot_nt(qs[hk], head_cols(k_ref, hk)) for hk in kv]
    m_prev = [m_ref[hk] for hk in kv]
    m_new = [jnp.maximum(m_prev[hk], jnp.max(ss[hk], axis=-1, keepdims=True)) for hk in kv]
    alpha = [jnp.exp2(m_prev[hk] - m_new[hk]) for hk in kv]
    ps = [jnp.exp2(ss[hk] - m_new[hk][:, 0:1]).astype(BF16) for hk in kv]
    pv = [_dot(ps[hk], vs[hk]) for hk in kv]
    for hk in kv:
        acc_ref[hk] = jnp.concatenate([alpha[hk], alpha[hk]], axis=1) * acc_ref[hk] + pv[hk]
        m_ref[hk] = m_new[hk]

    @pl.when(j == pl.num_programs(2) - 1)
    def _():
        for hk in kv:
            acc = acc_ref[hk]
            o = acc[:, :HEAD_DIM] / acc[:, HEAD_DIM:]
            for g in range(ATT_GROUPS):
                hh = hk * ATT_GROUPS + g
                o_ref[:, hh * HEAD_DIM:(hh + 1) * HEAD_DIM] = o[g * tq:(g + 1) * tq, :].astype(o_ref.dtype)


def _attention(q_rot, k_rot, p_main, v_off, tq=256, tk=2048):
    b, t, qw = q_rot.shape
    tk = min(tk, t)
    kvw = ATT_KV_HEADS * HEAD_DIM
    rows = ATT_GROUPS * tq
    return pl.pallas_call(
        _attn_kernel,
        grid=(b, t // tq, t // tk),
        in_specs=[
            pl.BlockSpec((None, tq, qw), lambda bi, i, j: (bi, i, 0)),
            pl.BlockSpec((None, tk, kvw), lambda bi, i, j: (bi, j, 0)),
            pl.BlockSpec((None, tk, kvw), lambda bi, i, j: (bi, j, v_off // kvw)),
        ],
        out_specs=pl.BlockSpec((None, tq, qw), lambda bi, i, j: (bi, i, 0)),
        out_shape=jax.ShapeDtypeStruct((b, t, qw), BF16),
        scratch_shapes=[
            pltpu.VMEM((ATT_KV_HEADS, rows, HEAD_DIM), F32),
            pltpu.VMEM((ATT_KV_HEADS, rows, 2 * HEAD_DIM), F32),
        ],
        compiler_params=_params(("arbitrary", "arbitrary", "arbitrary")),
    )(q_rot, k_rot, p_main)


def _out_proj_kernel(ya_ref, yb_ref, wa_ref, wb_ref, x_ref, gt_ref, o_ref):
    mix = _dot(ya_ref[...], wa_ref[...]) + _dot(yb_ref[...], wb_ref[...])
    o_ref[...] = x_ref[...] + gt_ref[...] * mix


def _out_proj(ya, yb, w_out, layer, x, gt, tm=512):
    b, t, d = x.shape
    ka = ya.shape[2]
    kb = yb.shape[2]
    return pl.pallas_call(
        _out_proj_kernel,
        grid=(b, t // tm),
        in_specs=[
            pl.BlockSpec((None, tm, ka), lambda bi, i: (bi, i, 0)),
            pl.BlockSpec((None, tm, kb), lambda bi, i: (bi, i, 0)),
            pl.BlockSpec((None, ka, d), lambda bi, i: (layer, 0, 0)),
            pl.BlockSpec((None, kb, d), lambda bi, i: (layer, ka // kb, 0)),
            pl.BlockSpec((None, tm, d), lambda bi, i: (bi, i, 0)),
            pl.BlockSpec((None, 1, d), lambda bi, i: (bi, 0, 0)),
        ],
        out_specs=pl.BlockSpec((None, tm, d), lambda bi, i: (bi, i, 0)),
        out_shape=jax.ShapeDtypeStruct((b, t, d), F32),
        compiler_params=_params(("arbitrary", "arbitrary")),
    )(ya, yb, w_out, w_out, x, gt)


def _mlp_kernel(x_ref, nw_ref, sc_ref, sh_ref, gt_ref, wu_ref, wd_ref, o_ref, h_ref, acc_ref):
    f = pl.program_id(2)

    @pl.when(f == 0)
    def _():
        h_ref[...] = _modulated_norm(x_ref[...], nw_ref[...], sc_ref[...], sh_ref[...]).astype(BF16)
        acc_ref[...] = jnp.zeros(acc_ref.shape, F32)

    u = jnp.maximum(_dot(h_ref[...], wu_ref[...]), 0.0)
    acc_ref[...] += _dot((u * u).astype(BF16), wd_ref[...])

    @pl.when(f == pl.num_programs(2) - 1)
    def _():
        o_ref[...] = x_ref[...] + gt_ref[...] * acc_ref[...]


def _mlp(x, nw, sc, sh, gt, w_up, w_down, layer, tm=512, tf=1024):
    b, t, d = x.shape
    dff = w_up.shape[2]
    vec = lambda bi, i, f: (bi, 0, 0)
    return pl.pallas_call(
        _mlp_kernel,
        grid=(b, t // tm, dff // tf),
        in_specs=[
            pl.BlockSpec((None, tm, d), lambda bi, i, f: (bi, i, 0)),
            pl.BlockSpec((1, d), lambda bi, i, f: (0, 0)),
            pl.BlockSpec((None, 1, d), vec),
            pl.BlockSpec((None, 1, d), vec),
            pl.BlockSpec((None, 1, d), vec),
            pl.BlockSpec((None, d, tf), lambda bi, i, f: (layer, 0, f)),
            pl.BlockSpec((None, tf, d), lambda bi, i, f: (layer, f, 0)),
        ],
        out_specs=pl.BlockSpec((None, tm, d), lambda bi, i, f: (bi, i, 0)),
        out_shape=jax.ShapeDtypeStruct((b, t, d), F32),
        scratch_shapes=[pltpu.VMEM((tm, d), BF16), pltpu.VMEM((tm, d), F32)],
        compiler_params=_params(("arbitrary", "arbitrary", "arbitrary")),
    )(x, nw, sc, sh, gt, w_up, w_down)


def _rope_tables(t):
    rows = t // GRID_W
    row = jnp.repeat(jnp.arange(rows, dtype=F32), GRID_W)
    col = jnp.tile(jnp.arange(GRID_W, dtype=F32), rows)
    inv_freq = ROPE_THETA ** (-jnp.arange(0, ROT_HALF, 2, dtype=F32) / ROT_HALF)
    ang_r = row[:, None] * inv_freq[None, :]
    ang_c = col[:, None] * inv_freq[None, :]
    ang = jnp.concatenate([ang_r, ang_r, ang_c, ang_c], axis=-1)
    return jnp.cos(ang), jnp.sin(ang)


def _gate_rows(a_log, dt_bias):
    def place(p):
        return jnp.pad(p.reshape(1, 2 * GDN_HEADS), ((0, 0), (2 * GDN_HEADS, HEAD_DIM - 4 * GDN_HEADS)))
    return place(a_log.astype(F32)), place(dt_bias.astype(F32))


def _trunk_layer(x, mod, layer, norm1_w, norm2_w, w_main, w_ba, conv_w, alog_row, dt_row, gdn_norm_w,
                 q_norm_w, k_norm_w, w_out, w_up, w_down, rope):
    b, t, d = x.shape
    sh1, sc1, gt1, sh2, sc2, gt2 = [m.reshape(b, 1, d) for m in jnp.split(mod, N_MOD, axis=-1)]
    gdn_w = GDN_HEADS * HEAD_DIM
    p_main, gates = _in_proj(x, norm1_w, sc1, sh1, w_main, w_ba, layer, alog_row, dt_row)
    ya = _gdn(p_main, gates, conv_w, gdn_norm_w)
    att_off = 4 * gdn_w
    q_rot, k_rot = _attn_prep(p_main, att_off, q_norm_w, k_norm_w, *rope)
    v_off = att_off + (ATT_HEADS + ATT_KV_HEADS) * HEAD_DIM
    yb = _attention(q_rot, k_rot, p_main, v_off)
    x = _out_proj(ya, yb, w_out, layer, x, gt1)
    return _mlp(x, norm2_w, sc2, sh2, gt2, w_up, w_down, layer)


def kernel(x_prompt, x_sample, c_prompt, c_sample, ada_w, ada_b, norm1_w, norm2_w, w_in, conv_w, a_log,
           dt_bias, gdn_norm_w, q_norm_w, k_norm_w, w_out, w_up, w_down):
    depth = ada_w.shape[0]
    gdn_w = GDN_HEADS * HEAD_DIM
    n_gate = 4 * GDN_HEADS
    nb_prompt = c_prompt.shape[0]

    mod = _adaln(jnp.concatenate([c_prompt, c_sample], axis=0), ada_w, ada_b)

    w_main = jnp.concatenate([w_in[:, :, :4 * gdn_w], w_in[:, :, 4 * gdn_w + n_gate:]], axis=-1).astype(BF16)
    w_ba = jnp.pad(w_in[:, :, 4 * gdn_w:4 * gdn_w + n_gate], ((0, 0), (0, 0), (0, HEAD_DIM - n_gate))).astype(BF16)
    w_out_b = w_out.astype(BF16)
    w_up_b = w_up.astype(BF16)
    w_down_b = w_down.astype(BF16)

    xs = [x_prompt, x_sample]
    ropes = [_rope_tables(x.shape[1]) for x in xs]
    for l in range(depth):
        alog_row, dt_row = _gate_rows(a_log[l], dt_bias[l])
        mods = [mod[l, :nb_prompt], mod[l, nb_prompt:]]
        for i in range(2):
            xs[i] = _trunk_layer(
                xs[i], mods[i], l, norm1_w[l].reshape(1, -1), norm2_w[l].reshape(1, -1), w_main, w_ba,
                conv_w[l], alog_row, dt_row, gdn_norm_w[l].reshape(1, -1), q_norm_w[l].reshape(1, -1),
                k_norm_w[l].reshape(1, -1), w_out_b, w_up_b, w_down_b, ropes[i])
    return (xs[0], xs[1])
```

```python
import functools

import jax
import jax.numpy as jnp
from jax import lax
from jax.experimental import pallas as pl
from jax.experimental.pallas import tpu as pltpu

F32 = jnp.float32
BF16 = jnp.bfloat16

HEAD_DIM = 128
GDN_HEADS = 8
ATT_HEADS = 8
ATT_KV_HEADS = 2
ATT_GROUPS = ATT_HEADS // ATT_KV_HEADS
CONV_K = 5
CHUNK = 64
PAIR = 2 * CHUNK
WIN = 4 * CHUNK
LHS_ROWS = HEAD_DIM + CHUNK
GRID_W = 64
ROPE_THETA = 10000.0
ROT_HALF = HEAD_DIM // 2
N_MOD = 6
EPS = 1e-6
LOG2_E = 1.4426950408889634
BASE_BLOCK = 8
WINDOWS_PER_STEP = 2
STEPS_PER_WINDOW_STEP = WINDOWS_PER_STEP * (WIN // CHUNK)
RING = 2 * STEPS_PER_WINDOW_STEP

VMEM_LIMIT = 56 * 1024 * 1024


def _sigmoid(x):
    return 1.0 / (1.0 + jnp.exp(-x))


def _silu(x):
    return x * _sigmoid(x)


def _softplus(x):
    return jnp.maximum(x, 0.0) + jnp.log(1.0 + jnp.exp(-jnp.abs(x)))


def _dot(a, b):
    return jnp.dot(a, b, preferred_element_type=F32)


def _dot_nt(a, b):
    return lax.dot_general(a, b, (((1,), (1,)), ((), ())), preferred_element_type=F32)


def _params(sem):
    return pltpu.CompilerParams(dimension_semantics=sem, vmem_limit_bytes=VMEM_LIMIT)


def _adaln_kernel(c_ref, w_ref, b_ref, o_ref):
    s = _silu(c_ref[...]).astype(BF16)
    o_ref[...] = _dot(s, w_ref[...].astype(BF16)) + b_ref[...]


def _adaln(c, ada_w, ada_b, tn=1024):
    depth, d, n = ada_w.shape
    rows = c.shape[0]
    return pl.pallas_call(
        _adaln_kernel,
        grid=(depth, n // tn),
        in_specs=[
            pl.BlockSpec((rows, d), lambda l, j: (0, 0)),
            pl.BlockSpec((None, d, tn), lambda l, j: (l, 0, j)),
            pl.BlockSpec((None, 1, tn), lambda l, j: (l, 0, j)),
        ],
        out_specs=pl.BlockSpec((None, rows, tn), lambda l, j: (l, 0, j)),
        out_shape=jax.ShapeDtypeStruct((depth, rows, n), F32),
        compiler_params=_params(("arbitrary", "arbitrary")),
    )(c, ada_w, ada_b.reshape(depth, 1, n))


def _modulated_norm(x, nw, sc, sh):
    y = x * lax.rsqrt(jnp.mean(x * x, axis=-1, keepdims=True) + EPS) * nw
    return y * (1.0 + sc) + sh


def _in_proj_kernel(x_ref, nw_ref, sc_ref, sh_ref, w_ref, wba_ref, alog_ref, dt_ref, o_ref, gate_ref, h_ref):
    @pl.when(pl.program_id(2) == 0)
    def _():
        h = _modulated_norm(x_ref[...], nw_ref[...], sc_ref[...], sh_ref[...]).astype(BF16)
        h_ref[...] = h
        ba = _dot(h, wba_ref[...])
        lane = lax.broadcasted_iota(jnp.int32, ba.shape, 1)
        gate_ref[...] = jnp.where(lane < 2 * GDN_HEADS, _sigmoid(ba),
                                  -jnp.exp(alog_ref[...]) * _softplus(ba + dt_ref[...]))

    o_ref[...] = _dot(h_ref[...], w_ref[...]).astype(o_ref.dtype)


def _in_proj(x, nw, sc, sh, w_all, layer, alog_row, dt_row, tm=512, tn=2816):
    b, t, d = x.shape
    nba = HEAD_DIM
    n = w_all.shape[2] - nba
    return pl.pallas_call(
        _in_proj_kernel,
        grid=(b, t // tm, n // tn),
        in_specs=[
            pl.BlockSpec((None, tm, d), lambda bi, i, j: (bi, i, 0)),
            pl.BlockSpec((1, d), lambda bi, i, j: (0, 0)),
            pl.BlockSpec((None, 1, d), lambda bi, i, j: (bi, 0, 0)),
            pl.BlockSpec((None, 1, d), lambda bi, i, j: (bi, 0, 0)),
            pl.BlockSpec((None, d, tn), lambda bi, i, j: (layer, 0, j)),
            pl.BlockSpec((None, d, nba), lambda bi, i, j: (layer, 0, n // nba)),
            pl.BlockSpec((1, nba), lambda bi, i, j: (0, 0)),
            pl.BlockSpec((1, nba), lambda bi, i, j: (0, 0)),
        ],
        out_specs=[
            pl.BlockSpec((None, tm, tn), lambda bi, i, j: (bi, i, j)),
            pl.BlockSpec((None, tm, nba), lambda bi, i, j: (bi, i, 0)),
        ],
        out_shape=[
            jax.ShapeDtypeStruct((b, t, n), BF16),
            jax.ShapeDtypeStruct((b, t, nba), F32),
        ],
        scratch_shapes=[pltpu.VMEM((tm, d), BF16)],
        compiler_params=_params(("arbitrary", "arbitrary", "arbitrary")),
    )(x, nw, sc, sh, w_all, w_all, alog_row, dt_row)


CONV_TILE = 256
CUM_TILES = 4
LANE_BETA = (0, 8)
LANE_G = (16, 24)


def _gdn_kernel(q_ref, k_ref, v_ref, z_ref, gate_ref, cwq_ref, cwk_ref, cwv_ref, nw_ref,
                o_ref,
                pad_ref, qs_ref, ks_ref, vs_ref, gsel_ref, gc_ref, gx_ref, grow_ref,
                lhs_ref, nb_ref, gl_ref, oacc_ref):
    t = q_ref.shape[0]
    n_tiles = t // CONV_TILE
    n_pairs = t // PAIR
    n_chunks = t // CHUNK
    n_win = t // WIN
    head = pl.program_id(1)

    halo = 8
    pad_ref[0:halo, :] = jnp.zeros((halo, HEAD_DIM), F32)
    pad_ref[halo + t:halo + t + halo, :] = jnp.zeros((halo, HEAD_DIM), F32)

    def conv_silu(src_ref, cw_ref, dst_ref, normalize, scale):
        def fill(i, c):
            r = pl.multiple_of(i * CONV_TILE, CONV_TILE)
            pad_ref[pl.ds(halo + r, CONV_TILE), :] = src_ref[pl.ds(r, CONV_TILE), :].astype(F32)
            return c

        lax.fori_loop(0, n_tiles, fill, 0)
        cw = cw_ref[...]

        def body(i, c):
            starts = [pl.multiple_of((2 * i + j) * CONV_TILE, CONV_TILE) for j in range(2)]
            accs = [None, None]
            for j in range(CONV_K):
                off = halo - CONV_K // 2 + j
                for n, r in enumerate(starts):
                    term = pad_ref[pl.ds(r + off, CONV_TILE), :] * cw[j:j + 1, :]
                    accs[n] = term if accs[n] is None else accs[n] + term
            ys = [_silu(acc) for acc in accs]
            if normalize:
                norms = [lax.rsqrt(jnp.sum(y * y, axis=-1, keepdims=True) + EPS) for y in ys]
                ys = [y * nrm for y, nrm in zip(ys, norms)]
            if scale != 1.0:
                ys = [y * scale for y in ys]
            for r, y in zip(starts, ys):
                dst_ref[pl.ds(r, CONV_TILE), :] = y
            return c

        lax.fori_loop(0, n_tiles // 2, body, 0)

    conv_silu(q_ref, cwq_ref, qs_ref, True, HEAD_DIM ** -0.5)
    conv_silu(k_ref, cwk_ref, ks_ref, True, 1.0)
    conv_silu(v_ref, cwv_ref, vs_ref, False, 1.0)

    ri = lax.broadcasted_iota(jnp.int32, (PAIR, PAIR), 0)
    ci = lax.broadcasted_iota(jnp.int32, (PAIR, PAIR), 1)
    same = (ri // CHUNK) == (ci // CHUNK)
    m_low = jnp.where(same & (ci <= ri), 1.0, 0.0).astype(BF16)
    m_up = jnp.where(same & (ci >= ri), 1.0, 0.0).astype(BF16)
    m_cum = jnp.concatenate([m_low, m_up], axis=0)

    lane_shift = (HEAD_DIM - head) % HEAD_DIM

    def cum_body(step, c):
        tiles = [step * CUM_TILES + j for j in range(CUM_TILES)]
        rows = [pl.ds(pl.multiple_of(i * PAIR, PAIR), PAIR) for i in tiles]
        xs = [pltpu.roll(gate_ref[r, :], lane_shift, 1) for r in rows]
        his = [x.astype(BF16) for x in xs]
        r1s = [x - hi.astype(F32) for x, hi in zip(xs, his)]
        mids = [r1.astype(BF16) for r1 in r1s]
        los = [(r1 - mid.astype(F32)).astype(BF16) for r1, mid in zip(r1s, mids)]
        cums = [_dot(m_cum, jnp.concatenate(parts, axis=1)) for parts in zip(his, mids, los)]
        bwd_lane = lax.broadcasted_iota(jnp.int32, (PAIR, HEAD_DIM), 1) == LANE_G[1]
        for i, r, x, cum in zip(tiles, rows, xs, cums):
            cum = cum[:, :HEAD_DIM] + cum[:, HEAD_DIM:2 * HEAD_DIM] + cum[:, 2 * HEAD_DIM:]
            pre, suf = cum[:PAIR], cum[PAIR:]
            gc = jnp.where(bwd_lane, suf, pre)
            gsel_ref[r, :] = x
            gc_ref[r, :] = gc
            gx_ref[r, :] = jnp.where(bwd_lane, pre, suf) - x
            gct = gc.T
            grow_ref[i, 0:1, :] = gct[LANE_G[0]:LANE_G[0] + 1, :]
            grow_ref[i, 1:2, :] = gct[LANE_G[1]:LANE_G[1] + 1, :]
        return c

    lax.fori_loop(0, n_pairs // CUM_TILES, cum_body, 0)

    def oacc_zero(i, c):
        r = pl.multiple_of(i * CONV_TILE, CONV_TILE)
        oacc_ref[pl.ds(r, CONV_TILE), :] = jnp.zeros((CONV_TILE, HEAD_DIM), F32)
        return c

    lax.fori_loop(0, n_tiles, oacc_zero, 0)

    def window_step(step):
        rw = lax.broadcasted_iota(jnp.int32, (WIN, WIN), 0)
        cw = lax.broadcasted_iota(jnp.int32, (WIN, WIN), 1)
        eye = jnp.where(rw == cw, 1.0, 0.0)

        def same_block(n):
            return (rw // n) == (cw // n)

        chains = []
        for wi in range(WINDOWS_PER_STEP):
            for d in range(2):
                w = step * WINDOWS_PER_STEP + wi
                if d == 1:
                    w = n_win - 1 - w
                rows = pl.ds(pl.multiple_of(w * WIN, WIN), WIN)
                q = qs_ref[rows, :]
                k = ks_ref[rows, :]
                v = vs_ref[rows, :]
                gs = gsel_ref[rows, :]
                gcv = gc_ref[rows, :]
                gxv = gx_ref[rows, :]
                growt = jnp.concatenate([grow_ref[2 * w], grow_ref[2 * w + 1]], axis=1)
                kbf = k.astype(BF16)
                beta = gs[:, LANE_BETA[d]:LANE_BETA[d] + 1]
                gcol = gcv[:, LANE_G[d]:LANE_G[d] + 1]
                eg = jnp.exp(gcol)
                kb = k * beta
                incl = same_block(CHUNK) & ((rw >= cw) if d == 0 else (rw <= cw))
                decay = jnp.where(incl, jnp.exp(jnp.where(incl, gcol - growt[d:d + 1, :], 0.0)), 0.0)
                gram = _dot_nt(jnp.concatenate([kb, q], axis=0).astype(BF16), kbf)
                chains.append(dict(
                    w=w, d=d, rows=rows, gcol=gcol,
                    rhs=jnp.concatenate([v * beta, kb * eg], axis=1).astype(BF16),
                    qd=q * eg,
                    kd=k * jnp.exp(gxv[:, LANE_G[d]:LANE_G[d] + 1]),
                    lmat=jnp.where(rw != cw, gram[:WIN] * decay, 0.0),
                    amat=(gram[WIN:] * decay).astype(BF16)))

        base = same_block(BASE_BLOCK)
        lb = [jnp.where(base, ch["lmat"], 0.0) for ch in chains]
        lbb = [x.astype(BF16) for x in lb]
        p2 = [_dot(x, x) for x in lbb]
        p2b = [x.astype(BF16) for x in p2]
        p4 = [_dot(x, x) for x in p2b]
        r1 = [(eye - x) + _dot((eye - x).astype(BF16), s) for x, s in zip(lb, p2b)]
        dinv = [r + _dot(r.astype(BF16), s.astype(BF16)) for r, s in zip(r1, p4)]
        n = BASE_BLOCK
        while n < CHUNK:
            off = same_block(2 * n) & jnp.logical_not(same_block(n))
            lo = [jnp.where(off, ch["lmat"], 0.0).astype(BF16) for ch in chains]
            db = [x.astype(BF16) for x in dinv]
            g = WIN // (2 * n)
            split = [x.reshape(g, 2 * n, WIN) for x in dinv]
            fwd = [ch["d"] == 0 for ch in chains]
            moving = [(s[:, n:, :] if f else s[:, :n, :]).reshape(WIN // 2, WIN) for s, f in zip(split, fwd)]
            xs = [_dot(a.astype(BF16), b) for a, b in zip(moving, lo)]
            moved = [(a - _dot(x.astype(BF16), b)).reshape(g, n, WIN) for a, x, b in zip(moving, xs, db)]
            dinv = [jnp.concatenate([s[:, :n, :], mv] if f else [mv, s[:, n:, :]], axis=1).reshape(WIN, WIN)
                    for s, mv, f in zip(split, moved, fwd)]
            n *= 2

        colchunk = lax.broadcasted_iota(jnp.int32, (HEAD_DIM, WIN), 1) // CHUNK
        uws = [_dot(a.astype(BF16), ch["rhs"]).astype(BF16) for a, ch in zip(dinv, chains)]
        auws = [_dot(ch["amat"], uw) for ch, uw in zip(chains, uws)]
        mns = []
        for ch, uw in zip(chains, uws):
            kd = ch["kd"]
            kdt = jnp.concatenate([kd[:PAIR].T, kd[PAIR:].T], axis=1)
            lhs4 = jnp.concatenate([jnp.where(colchunk == j, kdt, 0.0) for j in range(WIN // CHUNK)],
                                   axis=0).astype(BF16)
            mns.append(_dot(lhs4, uw))
        for ch, auw, mn in zip(chains, auws, mns):
            d = ch["d"]
            oacc_ref[ch["rows"], :] += auw[:, :HEAD_DIM]
            qp = (ch["qd"] - auw[:, HEAD_DIM:]).astype(BF16)
            for j in range(WIN // CHUNK):
                cidx = (WIN // CHUNK) * ch["w"] + j
                slot = (cidx if d == 0 else n_chunks - 1 - cidx) % RING
                blk = mn[j * HEAD_DIM:(j + 1) * HEAD_DIM]
                lo_row = d * LHS_ROWS
                lhs_ref[slot, lo_row:lo_row + HEAD_DIM, :] = (-blk[:, HEAD_DIM:]).astype(BF16)
                lhs_ref[slot, lo_row + HEAD_DIM:lo_row + LHS_ROWS, :] = qp[j * CHUNK:(j + 1) * CHUNK]
                nb_ref[slot, :, d * HEAD_DIM:(d + 1) * HEAD_DIM] = blk[:, :HEAD_DIM]
                last = j * CHUNK + (CHUNK - 1 if d == 0 else 0)
                gl_ref[slot, :, d * HEAD_DIM:(d + 1) * HEAD_DIM] = jnp.broadcast_to(
                    jnp.exp(ch["gcol"][last:last + 1, :]), (8, HEAD_DIM))

    def sweep_step(it, s):
        rf = pl.multiple_of(it * CHUNK, CHUNK)
        rb = pl.multiple_of((n_chunks - 1 - it) * CHUNK, CHUNK)
        slot = it % RING
        r = _dot(lhs_ref[slot], s.astype(BF16))
        upd = jnp.concatenate([r[0:HEAD_DIM, 0:HEAD_DIM],
                               r[LHS_ROWS:LHS_ROWS + HEAD_DIM, HEAD_DIM:]], axis=1)
        oacc_ref[pl.ds(rf, CHUNK), :] += r[HEAD_DIM:LHS_ROWS, 0:HEAD_DIM]
        oacc_ref[pl.ds(rb, CHUNK), :] += r[LHS_ROWS + HEAD_DIM:, HEAD_DIM:]
        return s * gl_ref[slot][0:1, :] + upd + nb_ref[slot]

    def sweep_block(step, s):
        for j in range(STEPS_PER_WINDOW_STEP):
            s = sweep_step(step * STEPS_PER_WINDOW_STEP + j, s)
        return s

    n_wsteps = n_win // WINDOWS_PER_STEP
    window_step(0)

    def pipelined_body(step, s):
        s = sweep_block(step - 1, s)
        window_step(step)
        return s

    state = lax.fori_loop(1, n_wsteps, pipelined_body, jnp.zeros((HEAD_DIM, 2 * HEAD_DIM), F32))
    sweep_block(n_wsteps - 1, state)

    nw = nw_ref[...]

    def out_body(i, c):
        r = pl.multiple_of(i * CONV_TILE, CONV_TILE)
        o = oacc_ref[pl.ds(r, CONV_TILE), :]
        y = o * lax.rsqrt(jnp.mean(o * o, axis=-1, keepdims=True) + EPS) * nw
        z = z_ref[pl.ds(r, CONV_TILE), :].astype(F32)
        o_ref[pl.ds(r, CONV_TILE), :] = (y * _silu(z)).astype(o_ref.dtype)
        return c

    lax.fori_loop(0, n_tiles, out_body, 0)


def _gdn(p_main, gates, conv_w, norm_w):
    b, t, _ = p_main.shape
    h = GDN_HEADS
    n_chunks = t // CHUNK
    n_pairs = t // PAIR
    col = lambda off: (lambda bi, hi: (bi, 0, off + hi))
    cwcol = lambda off: (lambda bi, hi: (0, off + hi))
    const2 = lambda bi, hi: (0, 0)
    return pl.pallas_call(
        _gdn_kernel,
        grid=(b, h),
        in_specs=[
            pl.BlockSpec((None, t, HEAD_DIM), col(0)),
            pl.BlockSpec((None, t, HEAD_DIM), col(h)),
            pl.BlockSpec((None, t, HEAD_DIM), col(2 * h)),
            pl.BlockSpec((None, t, HEAD_DIM), col(3 * h)),
            pl.BlockSpec((None, t, HEAD_DIM), lambda bi, hi: (bi, 0, 0), pipeline_mode=pl.Buffered(1)),
            pl.BlockSpec((CONV_K, HEAD_DIM), cwcol(0)),
            pl.BlockSpec((CONV_K, HEAD_DIM), cwcol(h)),
            pl.BlockSpec((CONV_K, HEAD_DIM), cwcol(2 * h)),
            pl.BlockSpec((1, HEAD_DIM), const2),
        ],
        out_specs=pl.BlockSpec((None, t, HEAD_DIM), lambda bi, hi: (bi, 0, hi)),
        out_shape=jax.ShapeDtypeStruct((b, t, h * HEAD_DIM), BF16),
        scratch_shapes=[
            pltpu.VMEM((t + 16, HEAD_DIM), F32),
            pltpu.VMEM((t, HEAD_DIM), F32),
            pltpu.VMEM((t, HEAD_DIM), F32),
            pltpu.VMEM((t, HEAD_DIM), F32),
            pltpu.VMEM((t, HEAD_DIM), F32),
            pltpu.VMEM((t, HEAD_DIM), F32),
            pltpu.VMEM((t, HEAD_DIM), F32),
            pltpu.VMEM((n_pairs, 8, HEAD_DIM), F32),
            pltpu.VMEM((RING, 2 * LHS_ROWS, HEAD_DIM), BF16),
            pltpu.VMEM((RING, HEAD_DIM, 2 * HEAD_DIM), F32),
            pltpu.VMEM((RING, 8, 2 * HEAD_DIM), F32),
            pltpu.VMEM((t, HEAD_DIM), F32),
        ],
        compiler_params=_params(("arbitrary", "arbitrary")),
    )(p_main, p_main, p_main, p_main, gates, conv_w, conv_w, conv_w, norm_w)


def _attn_prep_kernel(q_ref, kv_ref, qw_ref, kw_ref, cos_ref, sin_ref, qo_ref, ko_ref):
    cos = cos_ref[...]
    sin = sin_ref[...]
    quarter = ROT_HALF // 2
    src = lax.broadcasted_iota(jnp.int32, (HEAD_DIM, HEAD_DIM), 0)
    dst = lax.broadcasted_iota(jnp.int32, (HEAD_DIM, HEAD_DIM), 1)
    dst_first = (dst % ROT_HALF) < quarter
    perm = jnp.where(dst_first & (src == dst + quarter), -1.0,
                     jnp.where(jnp.logical_not(dst_first) & (src == dst - quarter), 1.0, 0.0)).astype(BF16)
    lane_first = (lax.broadcasted_iota(jnp.int32, (1, HEAD_DIM), 1) % ROT_HALF) < quarter

    def tables(w, scale):
        w_rot = jnp.where(lane_first, pltpu.roll(w, HEAD_DIM - quarter, 1), pltpu.roll(w, quarter, 1))
        return (w * scale) * cos, (w_rot * scale) * sin

    def norm_rope(xb, wc, ws):
        x = xb.astype(F32)
        rstd = lax.rsqrt(jnp.mean(x * x, axis=-1, keepdims=True) + EPS)
        return (x * wc + _dot(xb, perm) * ws) * rstd

    q_tabs = tables(qw_ref[...], LOG2_E * HEAD_DIM ** -0.5)
    k_tabs = tables(kw_ref[...], 1.0)
    for hh in range(ATT_HEADS):
        sl = slice(hh * HEAD_DIM, (hh + 1) * HEAD_DIM)
        qo_ref[:, sl] = norm_rope(q_ref[:, sl], *q_tabs).astype(qo_ref.dtype)
    for hh in range(ATT_KV_HEADS):
        sl = slice(hh * HEAD_DIM, (hh + 1) * HEAD_DIM)
        ko_ref[:, sl] = norm_rope(kv_ref[:, sl], *k_tabs).astype(ko_ref.dtype)


def _attn_prep(p_main, q_off, qw, kw, cos, sin, tq=256):
    b, t, _ = p_main.shape
    qwid = ATT_HEADS * HEAD_DIM
    kvwid = 2 * ATT_KV_HEADS * HEAD_DIM
    row = lambda bi, i: (i, 0)
    const2 = lambda bi, i: (0, 0)
    return pl.pallas_call(
        _attn_prep_kernel,
        grid=(b, t // tq),
        in_specs=[
            pl.BlockSpec((None, tq, qwid), lambda bi, i: (bi, i, q_off // qwid)),
            pl.BlockSpec((None, tq, kvwid), lambda bi, i: (bi, i, (q_off + qwid) // kvwid)),
            pl.BlockSpec((1, HEAD_DIM), const2),
            pl.BlockSpec((1, HEAD_DIM), const2),
            pl.BlockSpec((tq, HEAD_DIM), row),
            pl.BlockSpec((tq, HEAD_DIM), row),
        ],
        out_specs=[
            pl.BlockSpec((None, tq, qwid), lambda bi, i: (bi, i, 0)),
            pl.BlockSpec((None, tq, ATT_KV_HEADS * HEAD_DIM), lambda bi, i: (bi, i, 0)),
        ],
        out_shape=[
            jax.ShapeDtypeStruct((b, t, qwid), BF16),
            jax.ShapeDtypeStruct((b, t, ATT_KV_HEADS * HEAD_DIM), BF16),
        ],
        compiler_params=_params(("arbitrary", "arbitrary")),
    )(p_main, p_main, qw, kw, cos, sin)


def _attn_kernel(q_ref, k_ref, v_ref, o_ref, m_ref, acc_ref):
    j = pl.program_id(2)
    tq = q_ref.shape[0]
    tk = k_ref.shape[0]
    kv = range(ATT_KV_HEADS)

    @pl.when(j == 0)
    def _():
        m_ref[...] = jnp.full(m_ref.shape, jnp.finfo(F32).min, F32)
        acc_ref[...] = jnp.zeros(acc_ref.shape, F32)

    def head_cols(ref, hh):
        return ref[:, hh * HEAD_DIM:(hh + 1) * HEAD_DIM]

    ones = jnp.ones((tk, HEAD_DIM), BF16)
    qs = [jnp.concatenate([head_cols(q_ref, hk * ATT_GROUPS + g) for g in range(ATT_GROUPS)], axis=0) for hk in kv]
    vs = [jnp.concatenate([head_cols(v_ref, hk), ones], axis=1) for hk in kv]
    ss = [_dot_nt(qs[hk], head_cols(k_ref, hk)) for hk in kv]
    m_prev = [m_ref[hk] for hk in kv]
    m_new = [jnp.maximum(m_prev[hk], jnp.max(ss[hk], axis=-1, keepdims=True)) for hk in kv]
    alpha = [jnp.exp2(m_prev[hk] - m_new[hk]) for hk in kv]
    ps = [jnp.exp2(ss[hk] - m_new[hk][:, 0:1]).astype(BF16) for hk in kv]
    pv = [_dot(ps[hk], vs[hk]) for hk in kv]
    for hk in kv:
        acc_ref[hk] = jnp.concatenate([alpha[hk], alpha[hk]], axis=1) * acc_ref[hk] + pv[hk]
        m_ref[hk] = m_new[hk]

    @pl.when(j == pl.num_programs(2) - 1)
    def _():
        for hk in kv:
            acc = acc_ref[hk]
            o = acc[:, :HEAD_DIM] / acc[:, HEAD_DIM:]
            for g in range(ATT_GROUPS):
                hh = hk * ATT_GROUPS + g
                o_ref[:, hh * HEAD_DIM:(hh + 1) * HEAD_DIM] = o[g * tq:(g + 1) * tq, :].astype(o_ref.dtype)


def _attention(q_rot, k_rot, p_main, v_off, tq=256, tk=2048):
    b, t, qw = q_rot.shape
    tk = min(tk, t)
    assert t % tk == 0 and t % tq == 0
    kvw = ATT_KV_HEADS * HEAD_DIM
    rows = ATT_GROUPS * tq
    return pl.pallas_call(
        _attn_kernel,
        grid=(b, t // tq, t // tk),
        in_specs=[
            pl.BlockSpec((None, tq, qw), lambda bi, i, j: (bi, i, 0)),
            pl.BlockSpec((None, tk, kvw), lambda bi, i, j: (bi, j, 0)),
            pl.BlockSpec((None, tk, kvw), lambda bi, i, j: (bi, j, v_off // kvw)),
        ],
        out_specs=pl.BlockSpec((None, tq, qw), lambda bi, i, j: (bi, i, 0)),
        out_shape=jax.ShapeDtypeStruct((b, t, qw), BF16),
        scratch_shapes=[
            pltpu.VMEM((ATT_KV_HEADS, rows, HEAD_DIM), F32),
            pltpu.VMEM((ATT_KV_HEADS, rows, 2 * HEAD_DIM), F32),
        ],
        compiler_params=_params(("arbitrary", "arbitrary", "arbitrary")),
    )(q_rot, k_rot, p_main)


def _out_proj_kernel(ya_ref, yb_ref, wa_ref, wb_ref, x_ref, gt_ref, o_ref):
    mix = _dot(ya_ref[...], wa_ref[...]) + _dot(yb_ref[...], wb_ref[...])
    o_ref[...] = x_ref[...] + gt_ref[...] * mix


def _out_proj(ya, yb, w_out, layer, x, gt, tm=512):
    b, t, d = x.shape
    ka = ya.shape[2]
    kb = yb.shape[2]
    return pl.pallas_call(
        _out_proj_kernel,
        grid=(b, t // tm),
        in_specs=[
            pl.BlockSpec((None, tm, ka), lambda bi, i: (bi, i, 0)),
            pl.BlockSpec((None, tm, kb), lambda bi, i: (bi, i, 0)),
            pl.BlockSpec((None, ka, d), lambda bi, i: (layer, 0, 0)),
            pl.BlockSpec((None, kb, d), lambda bi, i: (layer, ka // kb, 0)),
            pl.BlockSpec((None, tm, d), lambda bi, i: (bi, i, 0)),
            pl.BlockSpec((None, 1, d), lambda bi, i: (bi, 0, 0)),
        ],
        out_specs=pl.BlockSpec((None, tm, d), lambda bi, i: (bi, i, 0)),
        out_shape=jax.ShapeDtypeStruct((b, t, d), F32),
        compiler_params=_params(("arbitrary", "arbitrary")),
    )(ya, yb, w_out, w_out, x, gt)


def _mlp_kernel(x_ref, nw_ref, sc_ref, sh_ref, gt_ref, wu_ref, wd_ref, o_ref, h_ref, acc_ref):
    f = pl.program_id(2)

    @pl.when(f == 0)
    def _():
        h_ref[...] = _modulated_norm(x_ref[...], nw_ref[...], sc_ref[...], sh_ref[...]).astype(BF16)
        acc_ref[...] = jnp.zeros(acc_ref.shape, F32)

    u = jnp.maximum(_dot(h_ref[...], wu_ref[...]), 0.0)
    acc_ref[...] += _dot((u * u).astype(BF16), wd_ref[...])

    @pl.when(f == pl.num_programs(2) - 1)
    def _():
        o_ref[...] = x_ref[...] + gt_ref[...] * acc_ref[...]


def _mlp(x, nw, sc, sh, gt, w_up, w_down, layer, tm=512, tf=1024):
    b, t, d = x.shape
    dff = w_up.shape[2]
    vec = lambda bi, i, f: (bi, 0, 0)
    return pl.pallas_call(
        _mlp_kernel,
        grid=(b, t // tm, dff // tf),
        in_specs=[
            pl.BlockSpec((None, tm, d), lambda bi, i, f: (bi, i, 0)),
            pl.BlockSpec((1, d), lambda bi, i, f: (0, 0)),
            pl.BlockSpec((None, 1, d), vec),
            pl.BlockSpec((None, 1, d), vec),
            pl.BlockSpec((None, 1, d), vec),
            pl.BlockSpec((None, d, tf), lambda bi, i, f: (layer, 0, f)),
            pl.BlockSpec((None, tf, d), lambda bi, i, f: (layer, f, 0)),
        ],
        out_specs=pl.BlockSpec((None, tm, d), lambda bi, i, f: (bi, i, 0)),
        out_shape=jax.ShapeDtypeStruct((b, t, d), F32),
        scratch_shapes=[pltpu.VMEM((tm, d), BF16), pltpu.VMEM((tm, d), F32)],
        compiler_params=_params(("arbitrary", "arbitrary", "arbitrary")),
    )(x, nw, sc, sh, gt, w_up, w_down)


def _rope_tables(t):
    rows = t // GRID_W
    row = jnp.repeat(jnp.arange(rows, dtype=F32), GRID_W)
    col = jnp.tile(jnp.arange(GRID_W, dtype=F32), rows)
    inv_freq = ROPE_THETA ** (-jnp.arange(0, ROT_HALF, 2, dtype=F32) / ROT_HALF)
    ang_r = row[:, None] * inv_freq[None, :]
    ang_c = col[:, None] * inv_freq[None, :]
    ang = jnp.concatenate([ang_r, ang_r, ang_c, ang_c], axis=-1)
    return jnp.cos(ang), jnp.sin(ang)


def _gate_rows(a_log, dt_bias):
    def place(p):
        return jnp.pad(p.reshape(1, 2 * GDN_HEADS), ((0, 0), (2 * GDN_HEADS, HEAD_DIM - 4 * GDN_HEADS)))
    return place(a_log.astype(F32)), place(dt_bias.astype(F32))


def _trunk_layer(x, mod, layer, norm1_w, norm2_w, w_all, conv_w, alog_row, dt_row, gdn_norm_w,
                 q_norm_w, k_norm_w, w_out, w_up, w_down, rope):
    b, t, d = x.shape
    sh1, sc1, gt1, sh2, sc2, gt2 = [m.reshape(b, 1, d) for m in jnp.split(mod, N_MOD, axis=-1)]
    gdn_w = GDN_HEADS * HEAD_DIM
    p_main, gates = _in_proj(x, norm1_w, sc1, sh1, w_all, layer, alog_row, dt_row)
    ya = _gdn(p_main, gates, conv_w, gdn_norm_w)
    att_off = 4 * gdn_w
    q_rot, k_rot = _attn_prep(p_main, att_off, q_norm_w, k_norm_w, *rope)
    v_off = att_off + (ATT_HEADS + ATT_KV_HEADS) * HEAD_DIM
    yb = _attention(q_rot, k_rot, p_main, v_off)
    x = _out_proj(ya, yb, w_out, layer, x, gt1)
    return _mlp(x, norm2_w, sc2, sh2, gt2, w_up, w_down, layer)


def kernel(x_prompt, x_sample, c_prompt, c_sample, ada_w, ada_b, norm1_w, norm2_w, w_in, conv_w, a_log,
           dt_bias, gdn_norm_w, q_norm_w, k_norm_w, w_out, w_up, w_down):
    depth = ada_w.shape[0]
    gdn_w = GDN_HEADS * HEAD_DIM
    n_gate = 4 * GDN_HEADS
    nb_prompt = c_prompt.shape[0]

    mod = _adaln(jnp.concatenate([c_prompt, c_sample], axis=0), ada_w, ada_b)

    w_all = jnp.concatenate(
        [w_in[:, :, :4 * gdn_w], w_in[:, :, 4 * gdn_w + n_gate:], w_in[:, :, 4 * gdn_w:4 * gdn_w + n_gate],
         jnp.zeros(w_in.shape[:2] + (HEAD_DIM - n_gate,), w_in.dtype)], axis=-1).astype(BF16)
    w_out_b = w_out.astype(BF16)
    w_up_b = w_up.astype(BF16)
    w_down_b = w_down.astype(BF16)

    xs = [x_prompt, x_sample]
    ropes = [_rope_tables(x.shape[1]) for x in xs]
    for l in range(depth):
        alog_row, dt_row = _gate_rows(a_log[l], dt_bias[l])
        mods = [mod[l, :nb_prompt], mod[l, nb_prompt:]]
        for i in range(2):
            xs[i] = _trunk_layer(
                xs[i], mods[i], l, norm1_w[l].reshape(1, -1), norm2_w[l].reshape(1, -1), w_all,
                conv_w[l], alog_row, dt_row, gdn_norm_w[l].reshape(1, -1), q_norm_w[l].reshape(1, -1),
                k_norm_w[l].reshape(1, -1), w_out_b, w_up_b, w_down_b, ropes[i])
    return (xs[0], xs[1])
```

```python
import functools

import jax
import jax.numpy as jnp
from jax import lax
from jax.experimental import pallas as pl
from jax.experimental.pallas import tpu as pltpu

F32 = jnp.float32
BF16 = jnp.bfloat16

HEAD_DIM = 128
GDN_HEADS = 8
ATT_HEADS = 8
ATT_KV_HEADS = 2
ATT_GROUPS = ATT_HEADS // ATT_KV_HEADS
CONV_K = 5
CHUNK = 64
PAIR = 2 * CHUNK
WIN = 4 * CHUNK
LHS_ROWS = HEAD_DIM + CHUNK
GRID_W = 64
ROPE_THETA = 10000.0
ROT_HALF = HEAD_DIM // 2
N_MOD = 6
EPS = 1e-6
LOG2_E = 1.4426950408889634
BASE_BLOCK = 8
WINDOWS_PER_STEP = 2
STEPS_PER_WINDOW_STEP = WINDOWS_PER_STEP * (WIN // CHUNK)
RING = 2 * STEPS_PER_WINDOW_STEP

VMEM_LIMIT = 56 * 1024 * 1024


def _sigmoid(x):
    return 1.0 / (1.0 + jnp.exp(-x))


def _silu(x):
    return x * _sigmoid(x)


def _softplus(x):
    return jnp.maximum(x, 0.0) + jnp.log(1.0 + jnp.exp(-jnp.abs(x)))


def _dot(a, b):
    return jnp.dot(a, b, preferred_element_type=F32)


def _dot_nt(a, b):
    return lax.dot_general(a, b, (((1,), (1,)), ((), ())), preferred_element_type=F32)


def _params(sem):
    return pltpu.CompilerParams(dimension_semantics=sem, vmem_limit_bytes=VMEM_LIMIT)


def _adaln_kernel(c_ref, w_ref, b_ref, o_ref):
    s = _silu(c_ref[...]).astype(BF16)
    o_ref[...] = _dot(s, w_ref[...].astype(BF16)) + b_ref[...]


def _adaln(c, ada_w, ada_b, tn=1024):
    depth, d, n = ada_w.shape
    rows = c.shape[0]
    return pl.pallas_call(
        _adaln_kernel,
        grid=(depth, n // tn),
        in_specs=[
            pl.BlockSpec((rows, d), lambda l, j: (0, 0)),
            pl.BlockSpec((None, d, tn), lambda l, j: (l, 0, j)),
            pl.BlockSpec((None, 1, tn), lambda l, j: (l, 0, j)),
        ],
        out_specs=pl.BlockSpec((None, rows, tn), lambda l, j: (l, 0, j)),
        out_shape=jax.ShapeDtypeStruct((depth, rows, n), F32),
        compiler_params=_params(("arbitrary", "arbitrary")),
    )(c, ada_w, ada_b.reshape(depth, 1, n))


def _modulated_norm(x, nw, sc, sh):
    y = x * lax.rsqrt(jnp.mean(x * x, axis=-1, keepdims=True) + EPS) * nw
    return y * (1.0 + sc) + sh


def _in_proj_kernel(x_ref, nw_ref, sc_ref, sh_ref, w_ref, wba_ref, alog_ref, dt_ref, o_ref, gate_ref, h_ref):
    @pl.when(pl.program_id(2) == 0)
    def _():
        h = _modulated_norm(x_ref[...], nw_ref[...], sc_ref[...], sh_ref[...]).astype(BF16)
        h_ref[...] = h
        ba = _dot(h, wba_ref[...])
        lane = lax.broadcasted_iota(jnp.int32, ba.shape, 1)
        gate_ref[...] = jnp.where(lane < 2 * GDN_HEADS, _sigmoid(ba),
                                  -jnp.exp(alog_ref[...]) * _softplus(ba + dt_ref[...]))

    o_ref[...] = _dot(h_ref[...], w_ref[...]).astype(o_ref.dtype)


def _in_proj(x, nw, sc, sh, w_main, w_ba, layer, alog_row, dt_row, tm=512, tn=2816):
    b, t, d = x.shape
    n = w_main.shape[2]
    nba = w_ba.shape[2]
    return pl.pallas_call(
        _in_proj_kernel,
        grid=(b, t // tm, n // tn),
        in_specs=[
            pl.BlockSpec((None, tm, d), lambda bi, i, j: (bi, i, 0)),
            pl.BlockSpec((1, d), lambda bi, i, j: (0, 0)),
            pl.BlockSpec((None, 1, d), lambda bi, i, j: (bi, 0, 0)),
            pl.BlockSpec((None, 1, d), lambda bi, i, j: (bi, 0, 0)),
            pl.BlockSpec((None, d, tn), lambda bi, i, j: (layer, 0, j)),
            pl.BlockSpec((None, d, nba), lambda bi, i, j: (layer, 0, 0)),
            pl.BlockSpec((1, nba), lambda bi, i, j: (0, 0)),
            pl.BlockSpec((1, nba), lambda bi, i, j: (0, 0)),
        ],
        out_specs=[
            pl.BlockSpec((None, tm, tn), lambda bi, i, j: (bi, i, j)),
            pl.BlockSpec((None, tm, nba), lambda bi, i, j: (bi, i, 0)),
        ],
        out_shape=[
            jax.ShapeDtypeStruct((b, t, n), BF16),
            jax.ShapeDtypeStruct((b, t, nba), F32),
        ],
        scratch_shapes=[pltpu.VMEM((tm, d), BF16)],
        compiler_params=_params(("arbitrary", "arbitrary", "arbitrary")),
    )(x, nw, sc, sh, w_main, w_ba, alog_row, dt_row)


CONV_TILE = 256
CUM_TILES = 4
LANE_BETA = (0, 8)
LANE_G = (16, 24)


def _gdn_kernel(q_ref, k_ref, v_ref, z_ref, gate_ref, cwq_ref, cwk_ref, cwv_ref, nw_ref,
                o_ref,
                pad_ref, qs_ref, ks_ref, vs_ref, gsel_ref, gc_ref, gx_ref, grow_ref,
                lhs_ref, nb_ref, gl_ref, oacc_ref):
    t = q_ref.shape[0]
    n_tiles = t // CONV_TILE
    n_pairs = t // PAIR
    n_chunks = t // CHUNK
    n_win = t // WIN
    head = pl.program_id(1)

    halo = 8
    pad_ref[0:halo, :] = jnp.zeros((halo, HEAD_DIM), F32)
    pad_ref[halo + t:halo + t + halo, :] = jnp.zeros((halo, HEAD_DIM), F32)

    def conv_silu(src_ref, cw_ref, dst_ref, normalize, scale):
        def fill(i, c):
            r = pl.multiple_of(i * CONV_TILE, CONV_TILE)
            pad_ref[pl.ds(halo + r, CONV_TILE), :] = src_ref[pl.ds(r, CONV_TILE), :].astype(F32)
            return c

        lax.fori_loop(0, n_tiles, fill, 0)
        cw = cw_ref[...]

        def body(i, c):
            starts = [pl.multiple_of((2 * i + j) * CONV_TILE, CONV_TILE) for j in range(2)]
            accs = [None, None]
            for j in range(CONV_K):
                off = halo - CONV_K // 2 + j
                for n, r in enumerate(starts):
                    term = pad_ref[pl.ds(r + off, CONV_TILE), :] * cw[j:j + 1, :]
                    accs[n] = term if accs[n] is None else accs[n] + term
            ys = [_silu(acc) for acc in accs]
            if normalize:
                norms = [lax.rsqrt(jnp.sum(y * y, axis=-1, keepdims=True) + EPS) for y in ys]
                ys = [y * nrm for y, nrm in zip(ys, norms)]
            if scale != 1.0:
                ys = [y * scale for y in ys]
            for r, y in zip(starts, ys):
                dst_ref[pl.ds(r, CONV_TILE), :] = y
            return c

        lax.fori_loop(0, n_tiles // 2, body, 0)

    conv_silu(q_ref, cwq_ref, qs_ref, True, HEAD_DIM ** -0.5)
    conv_silu(k_ref, cwk_ref, ks_ref, True, 1.0)
    conv_silu(v_ref, cwv_ref, vs_ref, False, 1.0)

    ri = lax.broadcasted_iota(jnp.int32, (PAIR, PAIR), 0)
    ci = lax.broadcasted_iota(jnp.int32, (PAIR, PAIR), 1)
    same = (ri // CHUNK) == (ci // CHUNK)
    m_low = jnp.where(same & (ci <= ri), 1.0, 0.0).astype(BF16)
    m_up = jnp.where(same & (ci >= ri), 1.0, 0.0).astype(BF16)
    m_cum = jnp.concatenate([m_low, m_up], axis=0)

    lane_shift = (HEAD_DIM - head) % HEAD_DIM

    def cum_body(step, c):
        tiles = [step * CUM_TILES + j for j in range(CUM_TILES)]
        rows = [pl.ds(pl.multiple_of(i * PAIR, PAIR), PAIR) for i in tiles]
        xs = [pltpu.roll(gate_ref[r, :], lane_shift, 1) for r in rows]
        his = [x.astype(BF16) for x in xs]
        r1s = [x - hi.astype(F32) for x, hi in zip(xs, his)]
        mids = [r1.astype(BF16) for r1 in r1s]
        los = [(r1 - mid.astype(F32)).astype(BF16) for r1, mid in zip(r1s, mids)]
        cums = [_dot(m_cum, jnp.concatenate(parts, axis=1)) for parts in zip(his, mids, los)]
        bwd_lane = lax.broadcasted_iota(jnp.int32, (PAIR, HEAD_DIM), 1) == LANE_G[1]
        for i, r, x, cum in zip(tiles, rows, xs, cums):
            cum = cum[:, :HEAD_DIM] + cum[:, HEAD_DIM:2 * HEAD_DIM] + cum[:, 2 * HEAD_DIM:]
            pre, suf = cum[:PAIR], cum[PAIR:]
            gc = jnp.where(bwd_lane, suf, pre)
            gsel_ref[r, :] = x
            gc_ref[r, :] = gc
            gx_ref[r, :] = jnp.where(bwd_lane, pre, suf) - x
            gct = gc.T
            grow_ref[i, 0:1, :] = gct[LANE_G[0]:LANE_G[0] + 1, :]
            grow_ref[i, 1:2, :] = gct[LANE_G[1]:LANE_G[1] + 1, :]
        return c

    lax.fori_loop(0, n_pairs // CUM_TILES, cum_body, 0)

    def oacc_zero(i, c):
        r = pl.multiple_of(i * CONV_TILE, CONV_TILE)
        oacc_ref[pl.ds(r, CONV_TILE), :] = jnp.zeros((CONV_TILE, HEAD_DIM), F32)
        return c

    lax.fori_loop(0, n_tiles, oacc_zero, 0)

    def window_step(step):
        rw = lax.broadcasted_iota(jnp.int32, (WIN, WIN), 0)
        cw = lax.broadcasted_iota(jnp.int32, (WIN, WIN), 1)
        eye = jnp.where(rw == cw, 1.0, 0.0)

        def same_block(n):
            return (rw // n) == (cw // n)

        chains = []
        for wi in range(WINDOWS_PER_STEP):
            for d in range(2):
                w = step * WINDOWS_PER_STEP + wi
                if d == 1:
                    w = n_win - 1 - w
                rows = pl.ds(pl.multiple_of(w * WIN, WIN), WIN)
                q = qs_ref[rows, :]
                k = ks_ref[rows, :]
                v = vs_ref[rows, :]
                gs = gsel_ref[rows, :]
                gcv = gc_ref[rows, :]
                gxv = gx_ref[rows, :]
                growt = jnp.concatenate([grow_ref[2 * w], grow_ref[2 * w + 1]], axis=1)
                kbf = k.astype(BF16)
                beta = gs[:, LANE_BETA[d]:LANE_BETA[d] + 1]
                gcol = gcv[:, LANE_G[d]:LANE_G[d] + 1]
                eg = jnp.exp(gcol)
                kb = k * beta
                incl = same_block(CHUNK) & ((rw >= cw) if d == 0 else (rw <= cw))
                decay = jnp.where(incl, jnp.exp(jnp.where(incl, gcol - growt[d:d + 1, :], 0.0)), 0.0)
                gram = _dot_nt(jnp.concatenate([kb, q], axis=0).astype(BF16), kbf)
                chains.append(dict(
                    w=w, d=d, rows=rows, gcol=gcol,
                    rhs=jnp.concatenate([v * beta, kb * eg], axis=1).astype(BF16),
                    qd=q * eg,
                    kd=k * jnp.exp(gxv[:, LANE_G[d]:LANE_G[d] + 1]),
                    lmat=jnp.where(rw != cw, gram[:WIN] * decay, 0.0),
                    amat=(gram[WIN:] * decay).astype(BF16)))

        base = same_block(BASE_BLOCK)
        lb = [jnp.where(base, ch["lmat"], 0.0) for ch in chains]
        lbb = [x.astype(BF16) for x in lb]
        p2 = [_dot(x, x) for x in lbb]
        p2b = [x.astype(BF16) for x in p2]
        p4 = [_dot(x, x) for x in p2b]
        r1 = [(eye - x) + _dot((eye - x).astype(BF16), s) for x, s in zip(lb, p2b)]
        dinv = [r + _dot(r.astype(BF16), s.astype(BF16)) for r, s in zip(r1, p4)]
        n = BASE_BLOCK
        while n < CHUNK:
            off = same_block(2 * n) & jnp.logical_not(same_block(n))
            lo = [jnp.where(off, ch["lmat"], 0.0).astype(BF16) for ch in chains]
            db = [x.astype(BF16) for x in dinv]
            g = WIN // (2 * n)
            split = [x.reshape(g, 2 * n, WIN) for x in dinv]
            fwd = [ch["d"] == 0 for ch in chains]
            moving = [(s[:, n:, :] if f else s[:, :n, :]).reshape(WIN // 2, WIN) for s, f in zip(split, fwd)]
            xs = [_dot(a.astype(BF16), b) for a, b in zip(moving, lo)]
            moved = [(a - _dot(x.astype(BF16), b)).reshape(g, n, WIN) for a, x, b in zip(moving, xs, db)]
            dinv = [jnp.concatenate([s[:, :n, :], mv] if f else [mv, s[:, n:, :]], axis=1).reshape(WIN, WIN)
                    for s, mv, f in zip(split, moved, fwd)]
            n *= 2

        colchunk = lax.broadcasted_iota(jnp.int32, (HEAD_DIM, WIN), 1) // CHUNK
        uws = [_dot(a.astype(BF16), ch["rhs"]).astype(BF16) for a, ch in zip(dinv, chains)]
        auws = [_dot(ch["amat"], uw) for ch, uw in zip(chains, uws)]
        mns = []
        for ch, uw in zip(chains, uws):
            kd = ch["kd"]
            kdt = jnp.concatenate([kd[:PAIR].T, kd[PAIR:].T], axis=1)
            lhs4 = jnp.concatenate([jnp.where(colchunk == j, kdt, 0.0) for j in range(WIN // CHUNK)],
                                   axis=0).astype(BF16)
            mns.append(_dot(lhs4, uw))
        for ch, auw, mn in zip(chains, auws, mns):
            d = ch["d"]
            oacc_ref[ch["rows"], :] += auw[:, :HEAD_DIM]
            qp = (ch["qd"] - auw[:, HEAD_DIM:]).astype(BF16)
            for j in range(WIN // CHUNK):
                cidx = (WIN // CHUNK) * ch["w"] + j
                slot = (cidx if d == 0 else n_chunks - 1 - cidx) % RING
                blk = mn[j * HEAD_DIM:(j + 1) * HEAD_DIM]
                lo_row = d * LHS_ROWS
                lhs_ref[slot, lo_row:lo_row + HEAD_DIM, :] = (-blk[:, HEAD_DIM:]).astype(BF16)
                lhs_ref[slot, lo_row + HEAD_DIM:lo_row + LHS_ROWS, :] = qp[j * CHUNK:(j + 1) * CHUNK]
                nb_ref[slot, :, d * HEAD_DIM:(d + 1) * HEAD_DIM] = blk[:, :HEAD_DIM]
                last = j * CHUNK + (CHUNK - 1 if d == 0 else 0)
                gl_ref[slot, :, d * HEAD_DIM:(d + 1) * HEAD_DIM] = jnp.broadcast_to(
                    jnp.exp(ch["gcol"][last:last + 1, :]), (8, HEAD_DIM))

    def sweep_step(it, s):
        rf = pl.multiple_of(it * CHUNK, CHUNK)
        rb = pl.multiple_of((n_chunks - 1 - it) * CHUNK, CHUNK)
        slot = it % RING
        r = _dot(lhs_ref[slot], s.astype(BF16))
        upd = jnp.concatenate([r[0:HEAD_DIM, 0:HEAD_DIM],
                               r[LHS_ROWS:LHS_ROWS + HEAD_DIM, HEAD_DIM:]], axis=1)
        oacc_ref[pl.ds(rf, CHUNK), :] += r[HEAD_DIM:LHS_ROWS, 0:HEAD_DIM]
        oacc_ref[pl.ds(rb, CHUNK), :] += r[LHS_ROWS + HEAD_DIM:, HEAD_DIM:]
        return s * gl_ref[slot][0:1, :] + upd + nb_ref[slot]

    def sweep_block(step, s):
        for j in range(STEPS_PER_WINDOW_STEP):
            s = sweep_step(step * STEPS_PER_WINDOW_STEP + j, s)
        return s

    n_wsteps = n_win // WINDOWS_PER_STEP
    window_step(0)

    def pipelined_body(step, s):
        s = sweep_block(step - 1, s)
        window_step(step)
        return s

    state = lax.fori_loop(1, n_wsteps, pipelined_body, jnp.zeros((HEAD_DIM, 2 * HEAD_DIM), F32))
    sweep_block(n_wsteps - 1, state)

    nw = nw_ref[...]

    def out_body(i, c):
        r = pl.multiple_of(i * CONV_TILE, CONV_TILE)
        o = oacc_ref[pl.ds(r, CONV_TILE), :]
        y = o * lax.rsqrt(jnp.mean(o * o, axis=-1, keepdims=True) + EPS) * nw
        z = z_ref[pl.ds(r, CONV_TILE), :].astype(F32)
        o_ref[pl.ds(r, CONV_TILE), :] = (y * _silu(z)).astype(o_ref.dtype)
        return c

    lax.fori_loop(0, n_tiles, out_body, 0)


def _gdn(p_main, gates, conv_w, norm_w):
    b, t, _ = p_main.shape
    h = GDN_HEADS
    n_chunks = t // CHUNK
    n_pairs = t // PAIR
    col = lambda off: (lambda bi, hi: (bi, 0, off + hi))
    cwcol = lambda off: (lambda bi, hi: (0, off + hi))
    const2 = lambda bi, hi: (0, 0)
    return pl.pallas_call(
        _gdn_kernel,
        grid=(b, h),
        in_specs=[
            pl.BlockSpec((None, t, HEAD_DIM), col(0)),
            pl.BlockSpec((None, t, HEAD_DIM), col(h)),
            pl.BlockSpec((None, t, HEAD_DIM), col(2 * h)),
            pl.BlockSpec((None, t, HEAD_DIM), col(3 * h)),
            pl.BlockSpec((None, t, HEAD_DIM), lambda bi, hi: (bi, 0, 0), pipeline_mode=pl.Buffered(1)),
            pl.BlockSpec((CONV_K, HEAD_DIM), cwcol(0)),
            pl.BlockSpec((CONV_K, HEAD_DIM), cwcol(h)),
            pl.BlockSpec((CONV_K, HEAD_DIM), cwcol(2 * h)),
            pl.BlockSpec((1, HEAD_DIM), const2),
        ],
        out_specs=pl.BlockSpec((None, t, HEAD_DIM), lambda bi, hi: (bi, 0, hi)),
        out_shape=jax.ShapeDtypeStruct((b, t, h * HEAD_DIM), BF16),
        scratch_shapes=[
            pltpu.VMEM((t + 16, HEAD_DIM), F32),
            pltpu.VMEM((t, HEAD_DIM), F32),
            pltpu.VMEM((t, HEAD_DIM), F32),
            pltpu.VMEM((t, HEAD_DIM), F32),
            pltpu.VMEM((t, HEAD_DIM), F32),
            pltpu.VMEM((t, HEAD_DIM), F32),
            pltpu.VMEM((t, HEAD_DIM), F32),
            pltpu.VMEM((n_pairs, 8, HEAD_DIM), F32),
            pltpu.VMEM((RING, 2 * LHS_ROWS, HEAD_DIM), BF16),
            pltpu.VMEM((RING, HEAD_DIM, 2 * HEAD_DIM), F32),
            pltpu.VMEM((RING, 8, 2 * HEAD_DIM), F32),
            pltpu.VMEM((t, HEAD_DIM), F32),
        ],
        compiler_params=_params(("arbitrary", "arbitrary")),
    )(p_main, p_main, p_main, p_main, gates, conv_w, conv_w, conv_w, norm_w)


def _attn_prep_kernel(q_ref, kv_ref, qw_ref, kw_ref, cos_ref, sin_ref, qo_ref, ko_ref):
    cos = cos_ref[...]
    sin = sin_ref[...]
    quarter = ROT_HALF // 2
    src = lax.broadcasted_iota(jnp.int32, (HEAD_DIM, HEAD_DIM), 0)
    dst = lax.broadcasted_iota(jnp.int32, (HEAD_DIM, HEAD_DIM), 1)
    dst_first = (dst % ROT_HALF) < quarter
    perm = jnp.where(dst_first & (src == dst + quarter), -1.0,
                     jnp.where(jnp.logical_not(dst_first) & (src == dst - quarter), 1.0, 0.0)).astype(BF16)
    lane_first = (lax.broadcasted_iota(jnp.int32, (1, HEAD_DIM), 1) % ROT_HALF) < quarter

    def tables(w, scale):
        w_rot = jnp.where(lane_first, pltpu.roll(w, HEAD_DIM - quarter, 1), pltpu.roll(w, quarter, 1))
        return (w * scale) * cos, (w_rot * scale) * sin

    def norm_rope(xb, wc, ws):
        x = xb.astype(F32)
        rstd = lax.rsqrt(jnp.mean(x * x, axis=-1, keepdims=True) + EPS)
        return (x * wc + _dot(xb, perm) * ws) * rstd

    q_tabs = tables(qw_ref[...], LOG2_E * HEAD_DIM ** -0.5)
    k_tabs = tables(kw_ref[...], 1.0)
    for hh in range(ATT_HEADS):
        sl = slice(hh * HEAD_DIM, (hh + 1) * HEAD_DIM)
        qo_ref[:, sl] = norm_rope(q_ref[:, sl], *q_tabs).astype(qo_ref.dtype)
    for hh in range(ATT_KV_HEADS):
        sl = slice(hh * HEAD_DIM, (hh + 1) * HEAD_DIM)
        ko_ref[:, sl] = norm_rope(kv_ref[:, sl], *k_tabs).astype(ko_ref.dtype)


def _attn_prep(p_main, q_off, qw, kw, cos, sin, tq=256):
    b, t, _ = p_main.shape
    qwid = ATT_HEADS * HEAD_DIM
    kvwid = 2 * ATT_KV_HEADS * HEAD_DIM
    row = lambda bi, i: (i, 0)
    const2 = lambda bi, i: (0, 0)
    return pl.pallas_call(
        _attn_prep_kernel,
        grid=(b, t // tq),
        in_specs=[
            pl.BlockSpec((None, tq, qwid), lambda bi, i: (bi, i, q_off // qwid)),
            pl.BlockSpec((None, tq, kvwid), lambda bi, i: (bi, i, (q_off + qwid) // kvwid)),
            pl.BlockSpec((1, HEAD_DIM), const2),
            pl.BlockSpec((1, HEAD_DIM), const2),
            pl.BlockSpec((tq, HEAD_DIM), row),
            pl.BlockSpec((tq, HEAD_DIM), row),
        ],
        out_specs=[
            pl.BlockSpec((None, tq, qwid), lambda bi, i: (bi, i, 0)),
            pl.BlockSpec((None, tq, ATT_KV_HEADS * HEAD_DIM), lambda bi, i: (bi, i, 0)),
        ],
        out_shape=[
            jax.ShapeDtypeStruct((b, t, qwid), BF16),
            jax.ShapeDtypeStruct((b, t, ATT_KV_HEADS * HEAD_DIM), BF16),
        ],
        compiler_params=_params(("arbitrary", "arbitrary")),
    )(p_main, p_main, qw, kw, cos, sin)


ATT_SCORE_ELEMS = 512 * 4096


def _attn_kernel(q_ref, k_ref, v_ref, o_ref):
    tq = q_ref.shape[0]
    t = k_ref.shape[0]
    kv = range(ATT_KV_HEADS)

    def head_cols(ref, hh):
        return ref[:, hh * HEAD_DIM:(hh + 1) * HEAD_DIM]

    ones = jnp.ones((t, HEAD_DIM), BF16)
    qs = [jnp.concatenate([head_cols(q_ref, hk * ATT_GROUPS + g) for g in range(ATT_GROUPS)], axis=0) for hk in kv]
    vs = [jnp.concatenate([head_cols(v_ref, hk), ones], axis=1) for hk in kv]
    ss = [_dot_nt(qs[hk], head_cols(k_ref, hk)) for hk in kv]
    ms = [jnp.max(s, axis=-1, keepdims=True) for s in ss]
    ps = [jnp.exp2(s - m).astype(BF16) for s, m in zip(ss, ms)]
    pv = [_dot(p, v) for p, v in zip(ps, vs)]
    for hk in kv:
        o = pv[hk][:, :HEAD_DIM] / pv[hk][:, HEAD_DIM:]
        for g in range(ATT_GROUPS):
            hh = hk * ATT_GROUPS + g
            o_ref[:, hh * HEAD_DIM:(hh + 1) * HEAD_DIM] = o[g * tq:(g + 1) * tq, :].astype(o_ref.dtype)


def _attention(q_rot, k_rot, p_main, v_off):
    b, t, qw = q_rot.shape
    tq = max(8, ATT_SCORE_ELEMS // (ATT_GROUPS * t))
    assert t % tq == 0
    kvw = ATT_KV_HEADS * HEAD_DIM
    return pl.pallas_call(
        _attn_kernel,
        grid=(b, t // tq),
        in_specs=[
            pl.BlockSpec((None, tq, qw), lambda bi, i: (bi, i, 0)),
            pl.BlockSpec((None, t, kvw), lambda bi, i: (bi, 0, 0)),
            pl.BlockSpec((None, t, kvw), lambda bi, i: (bi, 0, v_off // kvw)),
        ],
        out_specs=pl.BlockSpec((None, tq, qw), lambda bi, i: (bi, i, 0)),
        out_shape=jax.ShapeDtypeStruct((b, t, qw), BF16),
        compiler_params=_params(("arbitrary", "arbitrary")),
    )(q_rot, k_rot, p_main)


def _out_proj_kernel(ya_ref, yb_ref, wa_ref, wb_ref, x_ref, gt_ref, o_ref):
    mix = _dot(ya_ref[...], wa_ref[...]) + _dot(yb_ref[...], wb_ref[...])
    o_ref[...] = x_ref[...] + gt_ref[...] * mix


def _out_proj(ya, yb, w_out, layer, x, gt, tm=512):
    b, t, d = x.shape
    ka = ya.shape[2]
    kb = yb.shape[2]
    return pl.pallas_call(
        _out_proj_kernel,
        grid=(b, t // tm),
        in_specs=[
            pl.BlockSpec((None, tm, ka), lambda bi, i: (bi, i, 0)),
            pl.BlockSpec((None, tm, kb), lambda bi, i: (bi, i, 0)),
            pl.BlockSpec((None, ka, d), lambda bi, i: (layer, 0, 0)),
            pl.BlockSpec((None, kb, d), lambda bi, i: (layer, ka // kb, 0)),
            pl.BlockSpec((None, tm, d), lambda bi, i: (bi, i, 0)),
            pl.BlockSpec((None, 1, d), lambda bi, i: (bi, 0, 0)),
        ],
        out_specs=pl.BlockSpec((None, tm, d), lambda bi, i: (bi, i, 0)),
        out_shape=jax.ShapeDtypeStruct((b, t, d), F32),
        compiler_params=_params(("arbitrary", "arbitrary")),
    )(ya, yb, w_out, w_out, x, gt)


def _mlp_kernel(x_ref, nw_ref, sc_ref, sh_ref, gt_ref, wu_ref, wd_ref, o_ref, h_ref, acc_ref):
    f = pl.program_id(2)

    @pl.when(f == 0)
    def _():
        h_ref[...] = _modulated_norm(x_ref[...], nw_ref[...], sc_ref[...], sh_ref[...]).astype(BF16)
        acc_ref[...] = jnp.zeros(acc_ref.shape, F32)

    u = jnp.maximum(_dot(h_ref[...], wu_ref[...]), 0.0)
    acc_ref[...] += _dot((u * u).astype(BF16), wd_ref[...])

    @pl.when(f == pl.num_programs(2) - 1)
    def _():
        o_ref[...] = x_ref[...] + gt_ref[...] * acc_ref[...]


def _mlp(x, nw, sc, sh, gt, w_up, w_down, layer, tm=512, tf=1024):
    b, t, d = x.shape
    dff = w_up.shape[2]
    vec = lambda bi, i, f: (bi, 0, 0)
    return pl.pallas_call(
        _mlp_kernel,
        grid=(b, t // tm, dff // tf),
        in_specs=[
            pl.BlockSpec((None, tm, d), lambda bi, i, f: (bi, i, 0)),
            pl.BlockSpec((1, d), lambda bi, i, f: (0, 0)),
            pl.BlockSpec((None, 1, d), vec),
            pl.BlockSpec((None, 1, d), vec),
            pl.BlockSpec((None, 1, d), vec),
            pl.BlockSpec((None, d, tf), lambda bi, i, f: (layer, 0, f)),
            pl.BlockSpec((None, tf, d), lambda bi, i, f: (layer, f, 0)),
        ],
        out_specs=pl.BlockSpec((None, tm, d), lambda bi, i, f: (bi, i, 0)),
        out_shape=jax.ShapeDtypeStruct((b, t, d), F32),
        scratch_shapes=[pltpu.VMEM((tm, d), BF16), pltpu.VMEM((tm, d), F32)],
        compiler_params=_params(("arbitrary", "arbitrary", "arbitrary")),
    )(x, nw, sc, sh, gt, w_up, w_down)


def _rope_tables(t):
    rows = t // GRID_W
    row = jnp.repeat(jnp.arange(rows, dtype=F32), GRID_W)
    col = jnp.tile(jnp.arange(GRID_W, dtype=F32), rows)
    inv_freq = ROPE_THETA ** (-jnp.arange(0, ROT_HALF, 2, dtype=F32) / ROT_HALF)
    ang_r = row[:, None] * inv_freq[None, :]
    ang_c = col[:, None] * inv_freq[None, :]
    ang = jnp.concatenate([ang_r, ang_r, ang_c, ang_c], axis=-1)
    return jnp.cos(ang), jnp.sin(ang)


def _gate_rows(a_log, dt_bias):
    def place(p):
        return jnp.pad(p.reshape(1, 2 * GDN_HEADS), ((0, 0), (2 * GDN_HEADS, HEAD_DIM - 4 * GDN_HEADS)))
    return place(a_log.astype(F32)), place(dt_bias.astype(F32))


def _trunk_layer(x, mod, layer, norm1_w, norm2_w, w_main, w_ba, conv_w, alog_row, dt_row, gdn_norm_w,
                 q_norm_w, k_norm_w, w_out, w_up, w_down, rope):
    b, t, d = x.shape
    sh1, sc1, gt1, sh2, sc2, gt2 = [m.reshape(b, 1, d) for m in jnp.split(mod, N_MOD, axis=-1)]
    gdn_w = GDN_HEADS * HEAD_DIM
    p_main, gates = _in_proj(x, norm1_w, sc1, sh1, w_main, w_ba, layer, alog_row, dt_row)
    ya = _gdn(p_main, gates, conv_w, gdn_norm_w)
    att_off = 4 * gdn_w
    q_rot, k_rot = _attn_prep(p_main, att_off, q_norm_w, k_norm_w, *rope)
    v_off = att_off + (ATT_HEADS + ATT_KV_HEADS) * HEAD_DIM
    yb = _attention(q_rot, k_rot, p_main, v_off)
    x = _out_proj(ya, yb, w_out, layer, x, gt1)
    return _mlp(x, norm2_w, sc2, sh2, gt2, w_up, w_down, layer)


def kernel(x_prompt, x_sample, c_prompt, c_sample, ada_w, ada_b, norm1_w, norm2_w, w_in, conv_w, a_log,
           dt_bias, gdn_norm_w, q_norm_w, k_norm_w, w_out, w_up, w_down):
    depth = ada_w.shape[0]
    gdn_w = GDN_HEADS * HEAD_DIM
    n_gate = 4 * GDN_HEADS
    nb_prompt = c_prompt.shape[0]

    mod = _adaln(jnp.concatenate([c_prompt, c_sample], axis=0), ada_w, ada_b)

    w_main = jnp.concatenate([w_in[:, :, :4 * gdn_w], w_in[:, :, 4 * gdn_w + n_gate:]], axis=-1).astype(BF16)
    w_ba = jnp.pad(w_in[:, :, 4 * gdn_w:4 * gdn_w + n_gate], ((0, 0), (0, 0), (0, HEAD_DIM - n_gate))).astype(BF16)
    w_out_b = w_out.astype(BF16)
    w_up_b = w_up.astype(BF16)
    w_down_b = w_down.astype(BF16)

    xs = [x_prompt, x_sample]
    ropes = [_rope_tables(x.shape[1]) for x in xs]
    for l in range(depth):
        alog_row, dt_row = _gate_rows(a_log[l], dt_bias[l])
        mods = [mod[l, :nb_prompt], mod[l, nb_prompt:]]
        for i in range(2):
            xs[i] = _trunk_layer(
                xs[i], mods[i], l, norm1_w[l].reshape(1, -1), norm2_w[l].reshape(1, -1), w_main, w_ba,
                conv_w[l], alog_row, dt_row, gdn_norm_w[l].reshape(1, -1), q_norm_w[l].reshape(1, -1),
                k_norm_w[l].reshape(1, -1), w_out_b, w_up_b, w_down_b, ropes[i])
    return (xs[0], xs[1])
```

```python
import functools

import jax
import jax.numpy as jnp
from jax import lax
from jax.experimental import pallas as pl
from jax.experimental.pallas import tpu as pltpu

F32 = jnp.float32
BF16 = jnp.bfloat16

HEAD_DIM = 128
GDN_HEADS = 8
ATT_HEADS = 8
ATT_KV_HEADS = 2
ATT_GROUPS = ATT_HEADS // ATT_KV_HEADS
CONV_K = 5
CHUNK = 64
PAIR = 2 * CHUNK
WIN = 4 * CHUNK
LHS_ROWS = HEAD_DIM + CHUNK
GRID_W = 64
ROPE_THETA = 10000.0
ROT_HALF = HEAD_DIM // 2
N_MOD = 6
EPS = 1e-6
LOG2_E = 1.4426950408889634
BASE_BLOCK = 8
WINDOWS_PER_STEP = 2
STEPS_PER_WINDOW_STEP = WINDOWS_PER_STEP * (WIN // CHUNK)
RING = 2 * STEPS_PER_WINDOW_STEP

VMEM_LIMIT = 56 * 1024 * 1024


def _sigmoid(x):
    return 1.0 / (1.0 + jnp.exp(-x))


def _silu(x):
    return x * _sigmoid(x)


def _softplus(x):
    return jnp.maximum(x, 0.0) + jnp.log(1.0 + jnp.exp(-jnp.abs(x)))


def _dot(a, b):
    return jnp.dot(a, b, preferred_element_type=F32)


def _dot_nt(a, b):
    return lax.dot_general(a, b, (((1,), (1,)), ((), ())), preferred_element_type=F32)


def _params(sem):
    return pltpu.CompilerParams(dimension_semantics=sem, vmem_limit_bytes=VMEM_LIMIT)


def _adaln_kernel(c_ref, w_ref, b_ref, o_ref):
    s = _silu(c_ref[...]).astype(BF16)
    o_ref[...] = _dot(s, w_ref[...].astype(BF16)) + b_ref[...]


def _adaln(c, ada_w, ada_b, tn=1024):
    depth, d, n = ada_w.shape
    rows = c.shape[0]
    return pl.pallas_call(
        _adaln_kernel,
        grid=(depth, n // tn),
        in_specs=[
            pl.BlockSpec((rows, d), lambda l, j: (0, 0)),
            pl.BlockSpec((None, d, tn), lambda l, j: (l, 0, j)),
            pl.BlockSpec((None, 1, tn), lambda l, j: (l, 0, j)),
        ],
        out_specs=pl.BlockSpec((None, rows, tn), lambda l, j: (l, 0, j)),
        out_shape=jax.ShapeDtypeStruct((depth, rows, n), F32),
        compiler_params=_params(("arbitrary", "arbitrary")),
    )(c, ada_w, ada_b.reshape(depth, 1, n))


def _modulated_norm(x, nw, sc, sh):
    y = x * lax.rsqrt(jnp.mean(x * x, axis=-1, keepdims=True) + EPS) * nw
    return y * (1.0 + sc) + sh


def _in_proj_kernel(x_ref, nw_ref, sc_ref, sh_ref, w_ref, wba_ref, alog_ref, dt_ref, o_ref, gate_ref, h_ref):
    @pl.when(pl.program_id(2) == 0)
    def _():
        h = _modulated_norm(x_ref[...], nw_ref[...], sc_ref[...], sh_ref[...]).astype(BF16)
        h_ref[...] = h
        ba = _dot(h, wba_ref[...])
        lane = lax.broadcasted_iota(jnp.int32, ba.shape, 1)
        gate_ref[...] = jnp.where(lane < 2 * GDN_HEADS, _sigmoid(ba),
                                  -jnp.exp(alog_ref[...]) * _softplus(ba + dt_ref[...]))

    o_ref[...] = _dot(h_ref[...], w_ref[...]).astype(o_ref.dtype)


def _in_proj(x, nw, sc, sh, w_main, w_ba, layer, alog_row, dt_row, tm=512, tn=2816):
    b, t, d = x.shape
    n = w_main.shape[2]
    nba = w_ba.shape[2]
    return pl.pallas_call(
        _in_proj_kernel,
        grid=(b, t // tm, n // tn),
        in_specs=[
            pl.BlockSpec((None, tm, d), lambda bi, i, j: (bi, i, 0)),
            pl.BlockSpec((1, d), lambda bi, i, j: (0, 0)),
            pl.BlockSpec((None, 1, d), lambda bi, i, j: (bi, 0, 0)),
            pl.BlockSpec((None, 1, d), lambda bi, i, j: (bi, 0, 0)),
            pl.BlockSpec((None, d, tn), lambda bi, i, j: (layer, 0, j)),
            pl.BlockSpec((None, d, nba), lambda bi, i, j: (layer, 0, 0)),
            pl.BlockSpec((1, nba), lambda bi, i, j: (0, 0)),
            pl.BlockSpec((1, nba), lambda bi, i, j: (0, 0)),
        ],
        out_specs=[
            pl.BlockSpec((None, tm, tn), lambda bi, i, j: (bi, i, j)),
            pl.BlockSpec((None, tm, nba), lambda bi, i, j: (bi, i, 0)),
        ],
        out_shape=[
            jax.ShapeDtypeStruct((b, t, n), BF16),
            jax.ShapeDtypeStruct((b, t, nba), F32),
        ],
        scratch_shapes=[pltpu.VMEM((tm, d), BF16)],
        compiler_params=_params(("arbitrary", "arbitrary", "arbitrary")),
    )(x, nw, sc, sh, w_main, w_ba, alog_row, dt_row)


CONV_TILE = 256
CUM_TILES = 4
LANE_BETA = (0, 8)
LANE_G = (16, 24)


def _gdn_kernel(q_ref, k_ref, v_ref, z_ref, gate_ref, cwq_ref, cwk_ref, cwv_ref, nw_ref,
                o_ref,
                pad_ref, qs_ref, ks_ref, vs_ref, gsel_ref, gc_ref, gx_ref, grow_ref,
                lhs_ref, nb_ref, gl_ref, oacc_ref):
    t = q_ref.shape[0]
    n_tiles = t // CONV_TILE
    n_pairs = t // PAIR
    n_chunks = t // CHUNK
    n_win = t // WIN
    head = pl.program_id(1)

    halo = 8
    pad_ref[0:halo, :] = jnp.zeros((halo, HEAD_DIM), F32)
    pad_ref[halo + t:halo + t + halo, :] = jnp.zeros((halo, HEAD_DIM), F32)

    def conv_silu(src_ref, cw_ref, dst_ref, normalize, scale):
        def fill(i, c):
            r = pl.multiple_of(i * CONV_TILE, CONV_TILE)
            pad_ref[pl.ds(halo + r, CONV_TILE), :] = src_ref[pl.ds(r, CONV_TILE), :].astype(F32)
            return c

        lax.fori_loop(0, n_tiles, fill, 0)
        cw = cw_ref[...]

        def body(i, c):
            starts = [pl.multiple_of((2 * i + j) * CONV_TILE, CONV_TILE) for j in range(2)]
            accs = [None, None]
            for j in range(CONV_K):
                off = halo - CONV_K // 2 + j
                for n, r in enumerate(starts):
                    term = pad_ref[pl.ds(r + off, CONV_TILE), :] * cw[j:j + 1, :]
                    accs[n] = term if accs[n] is None else accs[n] + term
            ys = [_silu(acc) for acc in accs]
            if normalize:
                norms = [lax.rsqrt(jnp.sum(y * y, axis=-1, keepdims=True) + EPS) for y in ys]
                ys = [y * nrm for y, nrm in zip(ys, norms)]
            if scale != 1.0:
                ys = [y * scale for y in ys]
            for r, y in zip(starts, ys):
                dst_ref[pl.ds(r, CONV_TILE), :] = y
            return c

        lax.fori_loop(0, n_tiles // 2, body, 0)

    conv_silu(q_ref, cwq_ref, qs_ref, True, HEAD_DIM ** -0.5)
    conv_silu(k_ref, cwk_ref, ks_ref, True, 1.0)
    conv_silu(v_ref, cwv_ref, vs_ref, False, 1.0)

    ri = lax.broadcasted_iota(jnp.int32, (PAIR, PAIR), 0)
    ci = lax.broadcasted_iota(jnp.int32, (PAIR, PAIR), 1)
    same = (ri // CHUNK) == (ci // CHUNK)
    m_low = jnp.where(same & (ci <= ri), 1.0, 0.0).astype(BF16)
    m_up = jnp.where(same & (ci >= ri), 1.0, 0.0).astype(BF16)
    m_cum = jnp.concatenate([m_low, m_up], axis=0)

    lane_shift = (HEAD_DIM - head) % HEAD_DIM

    def cum_body(step, c):
        tiles = [step * CUM_TILES + j for j in range(CUM_TILES)]
        rows = [pl.ds(pl.multiple_of(i * PAIR, PAIR), PAIR) for i in tiles]
        xs = [pltpu.roll(gate_ref[r, :], lane_shift, 1) for r in rows]
        his = [x.astype(BF16) for x in xs]
        r1s = [x - hi.astype(F32) for x, hi in zip(xs, his)]
        mids = [r1.astype(BF16) for r1 in r1s]
        los = [(r1 - mid.astype(F32)).astype(BF16) for r1, mid in zip(r1s, mids)]
        cums = [_dot(m_cum, jnp.concatenate(parts, axis=1)) for parts in zip(his, mids, los)]
        bwd_lane = lax.broadcasted_iota(jnp.int32, (PAIR, HEAD_DIM), 1) == LANE_G[1]
        for i, r, x, cum in zip(tiles, rows, xs, cums):
            cum = cum[:, :HEAD_DIM] + cum[:, HEAD_DIM:2 * HEAD_DIM] + cum[:, 2 * HEAD_DIM:]
            pre, suf = cum[:PAIR], cum[PAIR:]
            gc = jnp.where(bwd_lane, suf, pre)
            gsel_ref[r, :] = x
            gc_ref[r, :] = gc
            gx_ref[r, :] = jnp.where(bwd_lane, pre, suf) - x
            gct = gc.T
            grow_ref[i, 0:1, :] = gct[LANE_G[0]:LANE_G[0] + 1, :]
            grow_ref[i, 1:2, :] = gct[LANE_G[1]:LANE_G[1] + 1, :]
        return c

    lax.fori_loop(0, n_pairs // CUM_TILES, cum_body, 0)

    def oacc_zero(i, c):
        r = pl.multiple_of(i * CONV_TILE, CONV_TILE)
        oacc_ref[pl.ds(r, CONV_TILE), :] = jnp.zeros((CONV_TILE, HEAD_DIM), F32)
        return c

    lax.fori_loop(0, n_tiles, oacc_zero, 0)

    def window_step(step):
        rw = lax.broadcasted_iota(jnp.int32, (WIN, WIN), 0)
        cw = lax.broadcasted_iota(jnp.int32, (WIN, WIN), 1)
        eye = jnp.where(rw == cw, 1.0, 0.0)

        def same_block(n):
            return (rw // n) == (cw // n)

        chains = []
        for wi in range(WINDOWS_PER_STEP):
            for d in range(2):
                w = step * WINDOWS_PER_STEP + wi
                if d == 1:
                    w = n_win - 1 - w
                rows = pl.ds(pl.multiple_of(w * WIN, WIN), WIN)
                q = qs_ref[rows, :]
                k = ks_ref[rows, :]
                v = vs_ref[rows, :]
                gs = gsel_ref[rows, :]
                gcv = gc_ref[rows, :]
                gxv = gx_ref[rows, :]
                growt = jnp.concatenate([grow_ref[2 * w], grow_ref[2 * w + 1]], axis=1)
                kbf = k.astype(BF16)
                beta = gs[:, LANE_BETA[d]:LANE_BETA[d] + 1]
                gcol = gcv[:, LANE_G[d]:LANE_G[d] + 1]
                eg = jnp.exp(gcol)
                kb = k * beta
                incl = same_block(CHUNK) & ((rw >= cw) if d == 0 else (rw <= cw))
                decay = jnp.where(incl, jnp.exp(jnp.where(incl, gcol - growt[d:d + 1, :], 0.0)), 0.0)
                gram = _dot_nt(jnp.concatenate([kb, q], axis=0).astype(BF16), kbf)
                chains.append(dict(
                    w=w, d=d, rows=rows, gcol=gcol,
                    rhs=jnp.concatenate([v * beta, kb * eg], axis=1).astype(BF16),
                    qd=q * eg,
                    kd=k * jnp.exp(gxv[:, LANE_G[d]:LANE_G[d] + 1]),
                    lmat=jnp.where(rw != cw, gram[:WIN] * decay, 0.0),
                    amat=(gram[WIN:] * decay).astype(BF16)))

        base = same_block(BASE_BLOCK)
        lb = [jnp.where(base, ch["lmat"], 0.0) for ch in chains]
        lbb = [x.astype(BF16) for x in lb]
        p2 = [_dot(x, x) for x in lbb]
        p2b = [x.astype(BF16) for x in p2]
        p4 = [_dot(x, x) for x in p2b]
        r1 = [(eye - x) + _dot((eye - x).astype(BF16), s) for x, s in zip(lb, p2b)]
        dinv = [r + _dot(r.astype(BF16), s.astype(BF16)) for r, s in zip(r1, p4)]
        n = BASE_BLOCK
        while n < CHUNK:
            off = same_block(2 * n) & jnp.logical_not(same_block(n))
            lo = [jnp.where(off, ch["lmat"], 0.0).astype(BF16) for ch in chains]
            db = [x.astype(BF16) for x in dinv]
            g = WIN // (2 * n)
            split = [x.reshape(g, 2 * n, WIN) for x in dinv]
            fwd = [ch["d"] == 0 for ch in chains]
            moving = [(s[:, n:, :] if f else s[:, :n, :]).reshape(WIN // 2, WIN) for s, f in zip(split, fwd)]
            xs = [_dot(a.astype(BF16), b) for a, b in zip(moving, lo)]
            moved = [(a - _dot(x.astype(BF16), b)).reshape(g, n, WIN) for a, x, b in zip(moving, xs, db)]
            dinv = [jnp.concatenate([s[:, :n, :], mv] if f else [mv, s[:, n:, :]], axis=1).reshape(WIN, WIN)
                    for s, mv, f in zip(split, moved, fwd)]
            n *= 2

        colchunk = lax.broadcasted_iota(jnp.int32, (HEAD_DIM, WIN), 1) // CHUNK
        uws = [_dot(a.astype(BF16), ch["rhs"]).astype(BF16) for a, ch in zip(dinv, chains)]
        auws = [_dot(ch["amat"], uw) for ch, uw in zip(chains, uws)]
        mns = []
        for ch, uw in zip(chains, uws):
            kd = ch["kd"]
            kdt = jnp.concatenate([kd[:PAIR].T, kd[PAIR:].T], axis=1)
            lhs4 = jnp.concatenate([jnp.where(colchunk == j, kdt, 0.0) for j in range(WIN // CHUNK)],
                                   axis=0).astype(BF16)
            mns.append(_dot(lhs4, uw))
        for ch, auw, mn in zip(chains, auws, mns):
            d = ch["d"]
            oacc_ref[ch["rows"], :] += auw[:, :HEAD_DIM]
            qp = (ch["qd"] - auw[:, HEAD_DIM:]).astype(BF16)
            for j in range(WIN // CHUNK):
                cidx = (WIN // CHUNK) * ch["w"] + j
                slot = (cidx if d == 0 else n_chunks - 1 - cidx) % RING
                blk = mn[j * HEAD_DIM:(j + 1) * HEAD_DIM]
                lo_row = d * LHS_ROWS
                lhs_ref[slot, lo_row:lo_row + HEAD_DIM, :] = (-blk[:, HEAD_DIM:]).astype(BF16)
                lhs_ref[slot, lo_row + HEAD_DIM:lo_row + LHS_ROWS, :] = qp[j * CHUNK:(j + 1) * CHUNK]
                nb_ref[slot, :, d * HEAD_DIM:(d + 1) * HEAD_DIM] = blk[:, :HEAD_DIM]
                last = j * CHUNK + (CHUNK - 1 if d == 0 else 0)
                gl_ref[slot, :, d * HEAD_DIM:(d + 1) * HEAD_DIM] = jnp.broadcast_to(
                    jnp.exp(ch["gcol"][last:last + 1, :]), (8, HEAD_DIM))

    def sweep_step(it, s):
        rf = pl.multiple_of(it * CHUNK, CHUNK)
        rb = pl.multiple_of((n_chunks - 1 - it) * CHUNK, CHUNK)
        slot = it % RING
        r = _dot(lhs_ref[slot], s.astype(BF16))
        upd = jnp.concatenate([r[0:HEAD_DIM, 0:HEAD_DIM],
                               r[LHS_ROWS:LHS_ROWS + HEAD_DIM, HEAD_DIM:]], axis=1)
        oacc_ref[pl.ds(rf, CHUNK), :] += r[HEAD_DIM:LHS_ROWS, 0:HEAD_DIM]
        oacc_ref[pl.ds(rb, CHUNK), :] += r[LHS_ROWS + HEAD_DIM:, HEAD_DIM:]
        return s * gl_ref[slot][0:1, :] + upd + nb_ref[slot]

    def sweep_block(step, s):
        for j in range(STEPS_PER_WINDOW_STEP):
            s = sweep_step(step * STEPS_PER_WINDOW_STEP + j, s)
        return s

    n_wsteps = n_win // WINDOWS_PER_STEP
    window_step(0)

    def pipelined_body(step, s):
        s = sweep_block(step - 1, s)
        window_step(step)
        return s

    state = lax.fori_loop(1, n_wsteps, pipelined_body, jnp.zeros((HEAD_DIM, 2 * HEAD_DIM), F32))
    sweep_block(n_wsteps - 1, state)

    nw = nw_ref[...]

    def out_body(i, c):
        r = pl.multiple_of(i * CONV_TILE, CONV_TILE)
        o = oacc_ref[pl.ds(r, CONV_TILE), :]
        y = o * lax.rsqrt(jnp.mean(o * o, axis=-1, keepdims=True) + EPS) * nw
        z = z_ref[pl.ds(r, CONV_TILE), :].astype(F32)
        o_ref[pl.ds(r, CONV_TILE), :] = (y * _silu(z)).astype(o_ref.dtype)
        return c

    lax.fori_loop(0, n_tiles, out_body, 0)


def _gdn(p_main, gates, conv_w, norm_w):
    b, t, _ = p_main.shape
    h = GDN_HEADS
    n_chunks = t // CHUNK
    n_pairs = t // PAIR
    col = lambda off: (lambda bi, hi: (bi, 0, off + hi))
    cwcol = lambda off: (lambda bi, hi: (0, off + hi))
    const2 = lambda bi, hi: (0, 0)
    return pl.pallas_call(
        _gdn_kernel,
        grid=(b, h),
        in_specs=[
            pl.BlockSpec((None, t, HEAD_DIM), col(0)),
            pl.BlockSpec((None, t, HEAD_DIM), col(h)),
            pl.BlockSpec((None, t, HEAD_DIM), col(2 * h)),
            pl.BlockSpec((None, t, HEAD_DIM), col(3 * h)),
            pl.BlockSpec((None, t, HEAD_DIM), lambda bi, hi: (bi, 0, 0), pipeline_mode=pl.Buffered(1)),
            pl.BlockSpec((CONV_K, HEAD_DIM), cwcol(0)),
            pl.BlockSpec((CONV_K, HEAD_DIM), cwcol(h)),
            pl.BlockSpec((CONV_K, HEAD_DIM), cwcol(2 * h)),
            pl.BlockSpec((1, HEAD_DIM), const2),
        ],
        out_specs=pl.BlockSpec((None, t, HEAD_DIM), lambda bi, hi: (bi, 0, hi)),
        out_shape=jax.ShapeDtypeStruct((b, t, h * HEAD_DIM), BF16),
        scratch_shapes=[
            pltpu.VMEM((t + 16, HEAD_DIM), F32),
            pltpu.VMEM((t, HEAD_DIM), F32),
            pltpu.VMEM((t, HEAD_DIM), F32),
            pltpu.VMEM((t, HEAD_DIM), F32),
            pltpu.VMEM((t, HEAD_DIM), F32),
            pltpu.VMEM((t, HEAD_DIM), F32),
            pltpu.VMEM((t, HEAD_DIM), F32),
            pltpu.VMEM((n_pairs, 8, HEAD_DIM), F32),
            pltpu.VMEM((RING, 2 * LHS_ROWS, HEAD_DIM), BF16),
            pltpu.VMEM((RING, HEAD_DIM, 2 * HEAD_DIM), F32),
            pltpu.VMEM((RING, 8, 2 * HEAD_DIM), F32),
            pltpu.VMEM((t, HEAD_DIM), F32),
        ],
        compiler_params=_params(("arbitrary", "arbitrary")),
    )(p_main, p_main, p_main, p_main, gates, conv_w, conv_w, conv_w, norm_w)


def _norm_rope_fn(w, scale, cos, sin):
    quarter = ROT_HALF // 2
    src = lax.broadcasted_iota(jnp.int32, (HEAD_DIM, HEAD_DIM), 0)
    dst = lax.broadcasted_iota(jnp.int32, (HEAD_DIM, HEAD_DIM), 1)
    dst_first = (dst % ROT_HALF) < quarter
    perm = jnp.where(dst_first & (src == dst + quarter), -1.0,
                     jnp.where(jnp.logical_not(dst_first) & (src == dst - quarter), 1.0, 0.0)).astype(BF16)
    lane_first = (lax.broadcasted_iota(jnp.int32, (1, HEAD_DIM), 1) % ROT_HALF) < quarter
    w_rot = jnp.where(lane_first, pltpu.roll(w, HEAD_DIM - quarter, 1), pltpu.roll(w, quarter, 1))
    wc = (w * scale) * cos
    ws = (w_rot * scale) * sin

    def norm_rope(xb):
        x = xb.astype(F32)
        rstd = lax.rsqrt(jnp.mean(x * x, axis=-1, keepdims=True) + EPS)
        return (x * wc + _dot(xb, perm) * ws) * rstd

    return norm_rope


def _k_prep_kernel(kv_ref, kw_ref, cos_ref, sin_ref, ko_ref):
    norm_rope = _norm_rope_fn(kw_ref[...], 1.0, cos_ref[...], sin_ref[...])
    for hh in range(ATT_KV_HEADS):
        sl = slice(hh * HEAD_DIM, (hh + 1) * HEAD_DIM)
        ko_ref[:, sl] = norm_rope(kv_ref[:, sl]).astype(ko_ref.dtype)


def _k_prep(p_main, k_off, kw, cos, sin, tq=512):
    b, t, _ = p_main.shape
    kvwid = 2 * ATT_KV_HEADS * HEAD_DIM
    row = lambda bi, i: (i, 0)
    return pl.pallas_call(
        _k_prep_kernel,
        grid=(b, t // tq),
        in_specs=[
            pl.BlockSpec((None, tq, kvwid), lambda bi, i: (bi, i, k_off // kvwid)),
            pl.BlockSpec((1, HEAD_DIM), lambda bi, i: (0, 0)),
            pl.BlockSpec((tq, HEAD_DIM), row),
            pl.BlockSpec((tq, HEAD_DIM), row),
        ],
        out_specs=pl.BlockSpec((None, tq, ATT_KV_HEADS * HEAD_DIM), lambda bi, i: (bi, i, 0)),
        out_shape=jax.ShapeDtypeStruct((b, t, ATT_KV_HEADS * HEAD_DIM), BF16),
        compiler_params=_params(("arbitrary", "arbitrary")),
    )(p_main, kw, cos, sin)


ATT_SCORE_ELEMS = 512 * 4096


def _attn_kernel(q_ref, qw_ref, cos_ref, sin_ref, k_ref, v_ref, o_ref):
    tq = q_ref.shape[0]
    t = k_ref.shape[0]
    kv = range(ATT_KV_HEADS)

    def head_cols(ref, hh):
        return ref[:, hh * HEAD_DIM:(hh + 1) * HEAD_DIM]

    norm_rope = _norm_rope_fn(qw_ref[...], LOG2_E * HEAD_DIM ** -0.5, cos_ref[...], sin_ref[...])
    ones = jnp.ones((t, HEAD_DIM), BF16)
    qs = [jnp.concatenate([norm_rope(head_cols(q_ref, hk * ATT_GROUPS + g)).astype(BF16)
                           for g in range(ATT_GROUPS)], axis=0) for hk in kv]
    vs = [jnp.concatenate([head_cols(v_ref, hk), ones], axis=1) for hk in kv]
    ss = [_dot_nt(qs[hk], head_cols(k_ref, hk)) for hk in kv]
    ms = [jnp.max(s, axis=-1, keepdims=True) for s in ss]
    ps = [jnp.exp2(s - m).astype(BF16) for s, m in zip(ss, ms)]
    pv = [_dot(p, v) for p, v in zip(ps, vs)]
    for hk in kv:
        o = pv[hk][:, :HEAD_DIM] / pv[hk][:, HEAD_DIM:]
        for g in range(ATT_GROUPS):
            hh = hk * ATT_GROUPS + g
            o_ref[:, hh * HEAD_DIM:(hh + 1) * HEAD_DIM] = o[g * tq:(g + 1) * tq, :].astype(o_ref.dtype)


def _attention(p_main, q_off, q_norm_w, cos, sin, k_rot, v_off):
    b, t, _ = p_main.shape
    qw = ATT_HEADS * HEAD_DIM
    tq = max(8, ATT_SCORE_ELEMS // (ATT_GROUPS * t))
    assert t % tq == 0
    kvw = ATT_KV_HEADS * HEAD_DIM
    return pl.pallas_call(
        _attn_kernel,
        grid=(b, t // tq),
        in_specs=[
            pl.BlockSpec((None, tq, qw), lambda bi, i: (bi, i, q_off // qw)),
            pl.BlockSpec((1, HEAD_DIM), lambda bi, i: (0, 0)),
            pl.BlockSpec((tq, HEAD_DIM), lambda bi, i: (i, 0)),
            pl.BlockSpec((tq, HEAD_DIM), lambda bi, i: (i, 0)),
            pl.BlockSpec((None, t, kvw), lambda bi, i: (bi, 0, 0)),
            pl.BlockSpec((None, t, kvw), lambda bi, i: (bi, 0, v_off // kvw)),
        ],
        out_specs=pl.BlockSpec((None, tq, qw), lambda bi, i: (bi, i, 0)),
        out_shape=jax.ShapeDtypeStruct((b, t, qw), BF16),
        compiler_params=_params(("arbitrary", "arbitrary")),
    )(p_main, q_norm_w, cos, sin, k_rot, p_main)


def _out_proj_kernel(ya_ref, yb_ref, wa_ref, wb_ref, x_ref, gt_ref, o_ref):
    mix = _dot(ya_ref[...], wa_ref[...]) + _dot(yb_ref[...], wb_ref[...])
    o_ref[...] = x_ref[...] + gt_ref[...] * mix


def _out_proj(ya, yb, w_out, layer, x, gt, tm=512):
    b, t, d = x.shape
    ka = ya.shape[2]
    kb = yb.shape[2]
    return pl.pallas_call(
        _out_proj_kernel,
        grid=(b, t // tm),
        in_specs=[
            pl.BlockSpec((None, tm, ka), lambda bi, i: (bi, i, 0)),
            pl.BlockSpec((None, tm, kb), lambda bi, i: (bi, i, 0)),
            pl.BlockSpec((None, ka, d), lambda bi, i: (layer, 0, 0)),
            pl.BlockSpec((None, kb, d), lambda bi, i: (layer, ka // kb, 0)),
            pl.BlockSpec((None, tm, d), lambda bi, i: (bi, i, 0)),
            pl.BlockSpec((None, 1, d), lambda bi, i: (bi, 0, 0)),
        ],
        out_specs=pl.BlockSpec((None, tm, d), lambda bi, i: (bi, i, 0)),
        out_shape=jax.ShapeDtypeStruct((b, t, d), F32),
        compiler_params=_params(("arbitrary", "arbitrary")),
    )(ya, yb, w_out, w_out, x, gt)


def _mlp_kernel(x_ref, nw_ref, sc_ref, sh_ref, gt_ref, wu_ref, wd_ref, o_ref, h_ref, acc_ref):
    f = pl.program_id(2)

    @pl.when(f == 0)
    def _():
        h_ref[...] = _modulated_norm(x_ref[...], nw_ref[...], sc_ref[...], sh_ref[...]).astype(BF16)
        acc_ref[...] = jnp.zeros(acc_ref.shape, F32)

    u = jnp.maximum(_dot(h_ref[...], wu_ref[...]), 0.0)
    acc_ref[...] += _dot((u * u).astype(BF16), wd_ref[...])

    @pl.when(f == pl.num_programs(2) - 1)
    def _():
        o_ref[...] = x_ref[...] + gt_ref[...] * acc_ref[...]


def _mlp(x, nw, sc, sh, gt, w_up, w_down, layer, tm=512, tf=1024):
    b, t, d = x.shape
    dff = w_up.shape[2]
    vec = lambda bi, i, f: (bi, 0, 0)
    return pl.pallas_call(
        _mlp_kernel,
        grid=(b, t // tm, dff // tf),
        in_specs=[
            pl.BlockSpec((None, tm, d), lambda bi, i, f: (bi, i, 0)),
            pl.BlockSpec((1, d), lambda bi, i, f: (0, 0)),
            pl.BlockSpec((None, 1, d), vec),
            pl.BlockSpec((None, 1, d), vec),
            pl.BlockSpec((None, 1, d), vec),
            pl.BlockSpec((None, d, tf), lambda bi, i, f: (layer, 0, f)),
            pl.BlockSpec((None, tf, d), lambda bi, i, f: (layer, f, 0)),
        ],
        out_specs=pl.BlockSpec((None, tm, d), lambda bi, i, f: (bi, i, 0)),
        out_shape=jax.ShapeDtypeStruct((b, t, d), F32),
        scratch_shapes=[pltpu.VMEM((tm, d), BF16), pltpu.VMEM((tm, d), F32)],
        compiler_params=_params(("arbitrary", "arbitrary", "arbitrary")),
    )(x, nw, sc, sh, gt, w_up, w_down)


def _rope_tables(t):
    rows = t // GRID_W
    row = jnp.repeat(jnp.arange(rows, dtype=F32), GRID_W)
    col = jnp.tile(jnp.arange(GRID_W, dtype=F32), rows)
    inv_freq = ROPE_THETA ** (-jnp.arange(0, ROT_HALF, 2, dtype=F32) / ROT_HALF)
    ang_r = row[:, None] * inv_freq[None, :]
    ang_c = col[:, None] * inv_freq[None, :]
    ang = jnp.concatenate([ang_r, ang_r, ang_c, ang_c], axis=-1)
    return jnp.cos(ang), jnp.sin(ang)


def _gate_rows(a_log, dt_bias):
    def place(p):
        return jnp.pad(p.reshape(1, 2 * GDN_HEADS), ((0, 0), (2 * GDN_HEADS, HEAD_DIM - 4 * GDN_HEADS)))
    return place(a_log.astype(F32)), place(dt_bias.astype(F32))


def _trunk_layer(x, mod, layer, norm1_w, norm2_w, w_main, w_ba, conv_w, alog_row, dt_row, gdn_norm_w,
                 q_norm_w, k_norm_w, w_out, w_up, w_down, rope):
    b, t, d = x.shape
    sh1, sc1, gt1, sh2, sc2, gt2 = [m.reshape(b, 1, d) for m in jnp.split(mod, N_MOD, axis=-1)]
    gdn_w = GDN_HEADS * HEAD_DIM
    p_main, gates = _in_proj(x, norm1_w, sc1, sh1, w_main, w_ba, layer, alog_row, dt_row)
    ya = _gdn(p_main, gates, conv_w, gdn_norm_w)
    att_off = 4 * gdn_w
    k_off = att_off + ATT_HEADS * HEAD_DIM
    v_off = k_off + ATT_KV_HEADS * HEAD_DIM
    k_rot = _k_prep(p_main, k_off, k_norm_w, *rope)
    yb = _attention(p_main, att_off, q_norm_w, *rope, k_rot, v_off)
    x = _out_proj(ya, yb, w_out, layer, x, gt1)
    return _mlp(x, norm2_w, sc2, sh2, gt2, w_up, w_down, layer)


def kernel(x_prompt, x_sample, c_prompt, c_sample, ada_w, ada_b, norm1_w, norm2_w, w_in, conv_w, a_log,
           dt_bias, gdn_norm_w, q_norm_w, k_norm_w, w_out, w_up, w_down):
    depth = ada_w.shape[0]
    gdn_w = GDN_HEADS * HEAD_DIM
    n_gate = 4 * GDN_HEADS
    nb_prompt = c_prompt.shape[0]

    mod = _adaln(jnp.concatenate([c_prompt, c_sample], axis=0), ada_w, ada_b)

    w_main = jnp.concatenate([w_in[:, :, :4 * gdn_w], w_in[:, :, 4 * gdn_w + n_gate:]], axis=-1).astype(BF16)
    w_ba = jnp.pad(w_in[:, :, 4 * gdn_w:4 * gdn_w + n_gate], ((0, 0), (0, 0), (0, HEAD_DIM - n_gate))).astype(BF16)
    w_out_b = w_out.astype(BF16)
    w_up_b = w_up.astype(BF16)
    w_down_b = w_down.astype(BF16)

    xs = [x_prompt, x_sample]
    ropes = [_rope_tables(x.shape[1]) for x in xs]
    for l in range(depth):
        alog_row, dt_row = _gate_rows(a_log[l], dt_bias[l])
        mods = [mod[l, :nb_prompt], mod[l, nb_prompt:]]
        for i in range(2):
            xs[i] = _trunk_layer(
                xs[i], mods[i], l, norm1_w[l].reshape(1, -1), norm2_w[l].reshape(1, -1), w_main, w_ba,
                conv_w[l], alog_row, dt_row, gdn_norm_w[l].reshape(1, -1), q_norm_w[l].reshape(1, -1),
                k_norm_w[l].reshape(1, -1), w_out_b, w_up_b, w_down_b, ropes[i])
    return (xs[0], xs[1])
```
